```python
import math
import jax, jax.numpy as jnp
from jax import lax
import numpy as np

D_MODEL = 1024
BATCH = 4
SEQ = 4096
DEPTH = 1

CHUNK = 64
MIX_WIDTH = D_MODEL
HGRN_WIDTH = MIX_WIDTH // 2
HGRN_HEAD_DIM = 128
HGRN_HEADS = HGRN_WIDTH // HGRN_HEAD_DIM
S5_WIDTH = MIX_WIDTH - HGRN_WIDTH
S5_GROUP = 16
S5_GROUPS = S5_WIDTH // S5_GROUP
S5_STATE = 64
IN_SPLITS = (HGRN_WIDTH, 2 * HGRN_WIDTH, 3 * HGRN_WIDTH, 4 * HGRN_WIDTH)
IN_COLS = 4 * HGRN_WIDTH + S5_WIDTH
N_MEM = 256
XA_HEADS = 4
XA_HEAD_DIM = D_MODEL // XA_HEADS
XA_SCALE = XA_HEAD_DIM ** -0.5
N_EXPERTS = 32
TOP_K = 4
D_FF = D_MODEL
SWIGLU_LIMIT = 7.0
SWIGLU_ALPHA = 1.702
MOE_BLOCK = 128
DEEPNORM_ALPHA = (2 * DEPTH) ** 0.25
DEEPNORM_BETA = (8 * DEPTH) ** -0.25
LN_EPS = 1e-5
RMS_EPS = 1e-6
S5_DT_MIN = 1e-3
S5_DT_MAX = 1e-1

kernel_name = "hgrn2_s5_xattn_moe_deepnorm_block"


def layer_norm(x, g, b):
    xf = x.astype(jnp.float32)
    mu = jnp.mean(xf, axis=-1, keepdims=True)
    var = jnp.mean(jnp.square(xf - mu), axis=-1, keepdims=True)
    return ((xf - mu) * lax.rsqrt(var + LN_EPS) * g + b).astype(x.dtype)


def hgrn2_mixer(q, f_logit, i_in, g, lb, norm_g):
    B, S, _ = q.shape
    nc = S // CHUNK
    f = lb + (1.0 - lb) * jax.nn.sigmoid(f_logit.astype(jnp.float32))
    log_f = jnp.log(f)
    k = 1.0 - f

    def heads(t):
        return t.reshape(B, nc, CHUNK, HGRN_HEADS, HGRN_HEAD_DIM).transpose(1, 0, 3, 2, 4)

    qc, kc, vc, lc = heads(q.astype(jnp.float32)), heads(k), heads(i_in.astype(jnp.float32)), heads(log_f)
    causal = jnp.tril(jnp.ones((CHUNK, CHUNK), dtype=bool))[:, :, None]

    def step(state, inp):
        qb, kb, vb, lfb = inp
        cum = jnp.cumsum(lfb, axis=2)
        diff = cum[:, :, :, None, :] - cum[:, :, None, :, :]
        decay = jnp.exp(jnp.where(causal, diff, -jnp.inf))
        scores = jnp.einsum('bhtd,bhsd,bhtsd->bhts', qb, kb, decay)
        intra = jnp.einsum('bhts,bhse->bhte', scores, vb)
        inter = jnp.einsum('bhtd,bhde->bhte', qb * jnp.exp(cum), state)
        last = cum[:, :, -1:, :]
        k_dec = kb * jnp.exp(last - cum)
        state = jnp.exp(last[:, :, 0, :])[..., None] * state + jnp.einsum('bhsd,bhse->bhde', k_dec, vb)
        return state, intra + inter

    state0 = jnp.zeros((B, HGRN_HEADS, HGRN_HEAD_DIM, HGRN_HEAD_DIM), jnp.float32)
    _, o = lax.scan(step, state0, (qc, kc, vc, lc))
    o = o.transpose(1, 0, 3, 2, 4).reshape(B, S, HGRN_HEADS, HGRN_HEAD_DIM)
    o = o * lax.rsqrt(jnp.mean(jnp.square(o), axis=-1, keepdims=True) + RMS_EPS) * norm_g
    return o.reshape(B, S, HGRN_WIDTH) * jax.nn.silu(g.astype(jnp.float32))


def s5_mixer(u, lam_re, lam_im, log_dt, b_re, b_im, c_re, c_im, d_skip, w_glu, b_glu):
    B, S, _ = u.shape
    uf = u.astype(jnp.float32)
    lam = lax.complex(lam_re.astype(jnp.float32), lam_im.astype(jnp.float32))
    dt = jnp.exp(log_dt.astype(jnp.float32))
    lam_bar = jnp.exp(lam * dt)
    b_mat = lax.complex(b_re.astype(jnp.float32), b_im.astype(jnp.float32))
    b_bar = ((lam_bar - 1.0) / lam)[..., None] * b_mat
    bu = jnp.einsum('gpn,bsgn->bsgp', b_bar, uf.reshape(B, S, S5_GROUPS, S5_GROUP))
    a = jnp.broadcast_to(lam_bar, bu.shape)

    def combine(left, right):
        a_l, b_l = left
        a_r, b_r = right
        return a_r * a_l, a_r * b_l + b_r

    _, states = lax.associative_scan(combine, (a, bu), axis=1)
    c_mat = lax.complex(c_re.astype(jnp.float32), c_im.astype(jnp.float32))
    y = jnp.einsum('gnp,bsgp->bsgn', c_mat, states).real.reshape(B, S, S5_WIDTH) + d_skip * uf
    y = jax.nn.gelu(y)
    return y * jax.nn.sigmoid(y @ w_glu.astype(jnp.float32) + b_glu)


def cross_attention(h, mem, w_q, w_k, w_v, w_o):
    B, S, D = h.shape
    q = (h @ w_q).reshape(B, S, XA_HEADS, XA_HEAD_DIM)
    k = (mem @ w_k).reshape(B, -1, XA_HEADS, XA_HEAD_DIM)
    v = (mem @ w_v).reshape(B, -1, XA_HEADS, XA_HEAD_DIM)
    s = jnp.einsum('bqhd,bkhd->bhqk', q, k).astype(jnp.float32) * XA_SCALE
    p = jax.nn.softmax(s, axis=-1).astype(v.dtype)
    o = jnp.einsum('bhqk,bkhd->bqhd', p, v).reshape(B, S, D)
    return o @ w_o


def moe_ffn(h, w_router, b_router, w1, b1, w2, b2):
    B, S, D = h.shape
    t = h.reshape(-1, D)
    T = t.shape[0]
    A = T * TOP_K
    logits = (t @ w_router + b_router).astype(jnp.float32)
    top_vals, top_idx = lax.top_k(logits, TOP_K)
    gates = jax.nn.softmax(top_vals, axis=-1)
    flat_e = top_idx.reshape(-1)
    flat_tok = jnp.repeat(jnp.arange(T, dtype=jnp.int32), TOP_K)
    flat_gate = gates.reshape(-1)
    order = jnp.argsort(flat_e)
    sorted_e = flat_e[order]
    counts = jnp.bincount(flat_e, length=N_EXPERTS)
    padded = (counts + MOE_BLOCK - 1) // MOE_BLOCK * MOE_BLOCK
    pad_end = jnp.cumsum(padded)
    pad_start = pad_end - padded
    grp_start = jnp.cumsum(counts) - counts
    dest = pad_start[sorted_e] + jnp.arange(A, dtype=jnp.int32) - grp_start[sorted_e]
    n_blocks = A // MOE_BLOCK + N_EXPERTS
    R = n_blocks * MOE_BLOCK
    row_tok = jnp.zeros((R,), jnp.int32).at[dest].set(flat_tok[order])
    row_gate = jnp.zeros((R,), jnp.float32).at[dest].set(flat_gate[order])
    block_start = jnp.arange(n_blocks, dtype=jnp.int32) * MOE_BLOCK
    block_expert = jnp.minimum(jnp.searchsorted(pad_end, block_start, side='right'), N_EXPERTS - 1)
    xb = t[row_tok].reshape(n_blocks, MOE_BLOCK, D)

    def expert_block(args):
        xe, e = args
        hh = xe @ w1[e] + b1[e]
        x_glu = jnp.minimum(hh[:, :D_FF], SWIGLU_LIMIT)
        x_lin = jnp.clip(hh[:, D_FF:], -SWIGLU_LIMIT, SWIGLU_LIMIT)
        act = x_glu * jax.nn.sigmoid(SWIGLU_ALPHA * x_glu) * (x_lin + 1.0)
        return act @ w2[e] + b2[e]

    yb = lax.map(expert_block, (xb, block_expert))
    y = yb.reshape(R, D).astype(jnp.float32) * row_gate[:, None]
    out = jnp.zeros((T, D), jnp.float32).at[row_tok].add(y)
    return out.reshape(B, S, D).astype(h.dtype)


def setup_inputs(seed: int = 0) -> dict:
    key = jax.random.key(seed)
    ks = jax.random.split(key, 40)
    L = DEPTH
    f32 = jnp.float32

    def nrm(k, shape, std):
        return jax.random.normal(k, shape, f32) * std

    def gain(k, shape):
        return 1.0 + 0.02 * jax.random.normal(k, shape, f32)

    n_idx = jnp.arange(S5_STATE, dtype=f32)
    return {
        "x": nrm(ks[0], (BATCH, SEQ, D_MODEL), 1.0),
        "mem": nrm(ks[1], (BATCH, N_MEM, D_MODEL), 1.0),
        "ln_in_g": gain(ks[2], (D_MODEL,)),
        "ln_in_b": nrm(ks[3], (D_MODEL,), 0.02),
        "w_in": nrm(ks[4], (L, D_MODEL, IN_COLS), D_MODEL ** -0.5),
        "hgrn_lb_logits": nrm(ks[5], (L + 1, HGRN_WIDTH), 0.5),
        "hgrn_norm_g": gain(ks[6], (L, HGRN_HEAD_DIM)),
        "s5_lam_re": -0.5 + 0.01 * jax.random.normal(ks[7], (L, S5_GROUPS, S5_STATE), f32),
        "s5_lam_im": math.pi * n_idx + 0.01 * jax.random.normal(ks[8], (L, S5_GROUPS, S5_STATE), f32),
        "s5_log_dt": jax.random.uniform(ks[9], (L, S5_GROUPS, S5_STATE), f32,
                                        minval=math.log(S5_DT_MIN), maxval=math.log(S5_DT_MAX)),
        "s5_b_re": nrm(ks[10], (L, S5_GROUPS, S5_STATE, S5_GROUP), (2 * S5_GROUP) ** -0.5),
        "s5_b_im": nrm(ks[11], (L, S5_GROUPS, S5_STATE, S5_GROUP), (2 * S5_GROUP) ** -0.5),
        "s5_c_re": nrm(ks[12], (L, S5_GROUPS, S5_GROUP, S5_STATE), (2 * S5_STATE) ** -0.5),
        "s5_c_im": nrm(ks[13], (L, S5_GROUPS, S5_GROUP, S5_STATE), (2 * S5_STATE) ** -0.5),
        "s5_d": nrm(ks[14], (L, S5_WIDTH), 1.0),
        "s5_w_glu": nrm(ks[15], (L, S5_WIDTH, S5_WIDTH), S5_WIDTH ** -0.5),
        "s5_b_glu": nrm(ks[16], (L, S5_WIDTH), 0.02),
        "w_out": nrm(ks[17], (L, MIX_WIDTH, D_MODEL), DEEPNORM_BETA * MIX_WIDTH ** -0.5),
        "ln1_g": gain(ks[18], (L, D_MODEL)),
        "ln1_b": nrm(ks[19], (L, D_MODEL), 0.02),
        "xa_w_q": nrm(ks[20], (L, D_MODEL, D_MODEL), D_MODEL ** -0.5),
        "xa_w_k": nrm(ks[21], (L, D_MODEL, D_MODEL), D_MODEL ** -0.5),
        "xa_w_v": nrm(ks[22], (L, D_MODEL, D_MODEL), D_MODEL ** -0.5),
        "xa_w_o": nrm(ks[23], (L, D_MODEL, D_MODEL), DEEPNORM_BETA * D_MODEL ** -0.5),
        "ln2_g": gain(ks[24], (L, D_MODEL)),
        "ln2_b": nrm(ks[25], (L, D_MODEL), 0.02),
        "router_w": nrm(ks[26], (L, D_MODEL, N_EXPERTS), D_MODEL ** -0.5),
        "router_b": nrm(ks[27], (L, N_EXPERTS), 0.01),
        "exp_w1": nrm(ks[28], (L, N_EXPERTS, D_MODEL, 2 * D_FF), D_MODEL ** -0.5),
        "exp_b1": nrm(ks[29], (L, N_EXPERTS, 2 * D_FF), 0.02),
        "exp_w2": nrm(ks[30], (L, N_EXPERTS, D_FF, D_MODEL), DEEPNORM_BETA * D_FF ** -0.5),
        "exp_b2": nrm(ks[31], (L, N_EXPERTS, D_MODEL), 0.02),
        "ln3_g": gain(ks[32], (L, D_MODEL)),
        "ln3_b": nrm(ks[33], (L, D_MODEL), 0.02),
    }


def reference(x, mem, ln_in_g, ln_in_b, w_in, hgrn_lb_logits, hgrn_norm_g, s5_lam_re, s5_lam_im,
              s5_log_dt, s5_b_re, s5_b_im, s5_c_re, s5_c_im, s5_d, s5_w_glu, s5_b_glu, w_out,
              ln1_g, ln1_b, xa_w_q, xa_w_k, xa_w_v, xa_w_o, ln2_g, ln2_b, router_w, router_b,
              exp_w1, exp_b1, exp_w2, exp_b2, ln3_g, ln3_b):
    h = layer_norm(x, ln_in_g, ln_in_b)
    lb_all = jnp.cumsum(jax.nn.softmax(hgrn_lb_logits.astype(jnp.float32), axis=0), axis=0)
    for l in range(DEPTH):
        proj = h @ w_in[l]
        q, f_logit, i_in, g, u = jnp.split(proj, IN_SPLITS, axis=-1)
        o_a = hgrn2_mixer(q, f_logit, i_in, g, lb_all[l], hgrn_norm_g[l])
        o_b = s5_mixer(u, s5_lam_re[l], s5_lam_im[l], s5_log_dt[l], s5_b_re[l], s5_b_im[l],
                       s5_c_re[l], s5_c_im[l], s5_d[l], s5_w_glu[l], s5_b_glu[l])
        mix = jnp.concatenate([o_a, o_b], axis=-1).astype(h.dtype) @ w_out[l]
        h = layer_norm(DEEPNORM_ALPHA * h + mix, ln1_g[l], ln1_b[l])
        xa = cross_attention(h, mem, xa_w_q[l], xa_w_k[l], xa_w_v[l], xa_w_o[l])
        h = layer_norm(DEEPNORM_ALPHA * h + xa, ln2_g[l], ln2_b[l])
        ff = moe_ffn(h, router_w[l], router_b[l], exp_w1[l], exp_b1[l], exp_w2[l], exp_b2[l])
        h = layer_norm(DEEPNORM_ALPHA * h + ff, ln3_g[l], ln3_b[l])
    return h
```

```python
import functools
import math

import jax
import jax.numpy as jnp
from jax import lax
from jax.experimental import pallas as pl
from jax.experimental.pallas import tpu as pltpu

F32 = jnp.float32
BF16 = jnp.bfloat16

D_MODEL = 1024
HG_W = 512
HG_D = 128
HG_H = HG_W // HG_D
HG_CHUNK = 64
HG_SUB = 16
S5_W = 512
S5_N = 16
S5_G = S5_W // S5_N
S5_P = 64
S5_L = 64
S5_TB = 8
IN_COLS = 4 * HG_W + S5_W
N_MEM = 256
XA_H = 4
XA_D = D_MODEL // XA_H
XA_SCALE = XA_D ** -0.5
N_EXP = 32
TOP_K = 4
D_FF = D_MODEL
SWIGLU_LIMIT = 7.0
SWIGLU_ALPHA = 1.702
DEPTH = 1
ALPHA = (2 * DEPTH) ** 0.25
LN_EPS = 1e-5
RMS_EPS = 1e-6
LANES = 128

TM_IN = 512
TQ_MID = 256
TM_MOE = 256
TK_TOK = 256
VMEM_LIMIT = 56 * 1024 * 1024


def _layer_norm(x, g, b):
    mu = jnp.mean(x, axis=-1, keepdims=True)
    xc = x - mu
    var = jnp.mean(xc * xc, axis=-1, keepdims=True)
    return xc * lax.rsqrt(var + LN_EPS) * g + b


def _dot(a, b):
    return jnp.dot(a, b, preferred_element_type=F32)


def _dot_nt(a, b):
    return lax.dot_general(a, b, (((1,), (1,)), ((), ())), preferred_element_type=F32)


def _dot_tn(a, b):
    return lax.dot_general(a, b, (((0,), (0,)), ((), ())), preferred_element_type=F32)


def _split3(x):
    hi = x.astype(BF16)
    r1 = x - hi.astype(F32)
    mid = r1.astype(BF16)
    lo = (r1 - mid.astype(F32)).astype(BF16)
    return hi, mid, lo


def _ln_inproj_kernel(x_ref, g_ref, b_ref, w_ref, h_ref, qfig_ref, u_ref):
    h = _layer_norm(x_ref[...], g_ref[...], b_ref[...])
    h_ref[...] = h
    p = _dot(h.astype(BF16), w_ref[...])
    qfig_ref[...] = p[:, :4 * HG_W]
    u_ref[...] = p[:, 4 * HG_W:]


def _ln_inproj(x2, g, b, w_bf):
    t = x2.shape[0]
    tm = min(TM_IN, t)
    return pl.pallas_call(
        _ln_inproj_kernel,
        grid=(t // tm,),
        in_specs=[
            pl.BlockSpec((tm, D_MODEL), lambda i: (i, 0)),
            pl.BlockSpec((1, D_MODEL), lambda i: (0, 0)),
            pl.BlockSpec((1, D_MODEL), lambda i: (0, 0)),
            pl.BlockSpec((D_MODEL, IN_COLS), lambda i: (0, 0)),
        ],
        out_specs=[
            pl.BlockSpec((tm, D_MODEL), lambda i: (i, 0)),
            pl.BlockSpec((tm, 4 * HG_W), lambda i: (i, 0)),
            pl.BlockSpec((tm, S5_W), lambda i: (i, 0)),
        ],
        out_shape=[
            jax.ShapeDtypeStruct((t, D_MODEL), F32),
            jax.ShapeDtypeStruct((t, 4 * HG_W), F32),
            jax.ShapeDtypeStruct((t, S5_W), F32),
        ],
        compiler_params=pltpu.CompilerParams(
            dimension_semantics=("arbitrary",), vmem_limit_bytes=VMEM_LIMIT),
        name="ln_inproj",
    )(x2, g, b, w_bf)


def _hgrn_kernel(q_ref, f_ref, i_ref, g_ref, lb_ref, ng_ref, o_ref, st_ref, k_s, cum_s):
    @pl.when(pl.program_id(1) == 0)
    def _():
        st_ref[...] = jnp.zeros_like(st_ref)

    lb = lb_ref[...]
    f = lb + (1.0 - lb) * jax.nn.sigmoid(f_ref[...])
    k_s[...] = 1.0 - f
    r = lax.broadcasted_iota(jnp.int32, (HG_CHUNK, HG_CHUNK), 0)
    c = lax.broadcasted_iota(jnp.int32, (HG_CHUNK, HG_CHUNK), 1)
    tri = (c <= r).astype(BF16)
    hi, mid, lo = _split3(jnp.log(f))
    cum_s[...] = _dot(tri, hi) + _dot(tri, mid) + _dot(tri, lo)

    row_in_sub = lax.broadcasted_iota(jnp.int32, (HG_SUB, 1), 0)
    ng = ng_ref[...]
    for h in range(HG_H):
        cols = slice(h * HG_D, (h + 1) * HG_D)
        qc = q_ref[:, cols]
        kc = k_s[:, cols]
        vc = i_ref[:, cols]
        cc = cum_s[:, cols]
        last = cum_s[HG_CHUNK - 1:HG_CHUNK, cols]
        st = st_ref[h]
        inter = _dot_nt((qc * jnp.exp(cc)).astype(BF16), st.astype(BF16))
        vb = vc.astype(BF16)
        outs = []
        for blk in range(HG_CHUNK // HG_SUB):
            rs = slice(HG_SUB * blk, HG_SUB * (blk + 1))
            q_i = qc[rs]
            c_i = cc[rs]
            acc = inter[rs]
            if blk > 0:
                n = HG_SUB * blk
                c_ref = cum_s[n - 1:n, cols]
                khat = (kc[:n] * jnp.exp(c_ref - cc[:n])).astype(BF16)
                qhat = (q_i * jnp.exp(c_i - c_ref)).astype(BF16)
                acc = acc + _dot(_dot_nt(qhat, khat).astype(BF16), vb[:n])
            for jj in range(HG_SUB):
                srow = HG_SUB * blk + jj
                c_s = cum_s[srow:srow + 1, cols]
                k_row = k_s[srow:srow + 1, cols]
                v_row = i_ref[srow:srow + 1, cols]
                p = q_i * k_row * jnp.exp(jnp.minimum(c_i - c_s, 0.0))
                w = jnp.sum(p, axis=-1, keepdims=True)
                w = jnp.where(row_in_sub >= jj, w, 0.0)
                acc = acc + w * v_row
            outs.append(acc)
        o = jnp.concatenate(outs, axis=0)
        kd = (kc * jnp.exp(last - cc)).astype(BF16)
        st_ref[h] = st * jnp.exp(last) + _dot_tn(vb, kd)
        o = o * lax.rsqrt(jnp.mean(o * o, axis=-1, keepdims=True) + RMS_EPS) * ng
        gg = g_ref[:, cols]
        o_ref[:, cols] = (o * (gg * jax.nn.sigmoid(gg))).astype(o_ref.dtype)


def _hgrn2(qfig, lb, ng, batch, seq):
    t = qfig.shape[0]
    rb = HG_CHUNK
    nj = seq // rb

    def col_spec(cb):
        return pl.BlockSpec((rb, HG_W), lambda b, j: (b * nj + j, cb))

    return pl.pallas_call(
        _hgrn_kernel,
        grid=(batch, nj),
        in_specs=[col_spec(0), col_spec(1), col_spec(2), col_spec(3),
                  pl.BlockSpec((1, HG_W), lambda b, j: (0, 0)),
                  pl.BlockSpec((1, HG_D), lambda b, j: (0, 0))],
        out_specs=pl.BlockSpec((rb, HG_W), lambda b, j: (b * nj + j, 0)),
        out_shape=jax.ShapeDtypeStruct((t, HG_W), BF16),
        scratch_shapes=[pltpu.VMEM((HG_H, HG_D, HG_D), F32),
                        pltpu.VMEM((rb, HG_W), F32),
                        pltpu.VMEM((rb, HG_W), F32)],
        compiler_params=pltpu.CompilerParams(
            dimension_semantics=("arbitrary", "arbitrary"), vmem_limit_bytes=VMEM_LIMIT),
        name="hgrn2",
    )(qfig, qfig, qfig, qfig, lb, ng)


def _s5_kernel(u_ref, tt_ref, hm_ref, gm_ref, ab_ref, d_ref, y_ref, *, nc):
    u = u_ref[0]
    rows = u.shape[0]
    ub = u.astype(BF16)
    nblk = S5_L // S5_TB
    ys = []
    for bc in range(nblk):
        ys.append(_dot(ub[:, :(bc + 1) * LANES], tt_ref[0, (nblk - 1 - bc) * LANES:, :]))
    y = jnp.concatenate(ys, axis=1)
    x = _dot(ub, hm_ref[0])
    chunk_id = lax.broadcasted_iota(jnp.int32, (rows, 2 * S5_P), 0) % nc
    lev = 0
    while (1 << lev) < nc:
        sh = 1 << lev
        a = ab_ref[0, 2 * lev:2 * lev + 1, :]
        bv = ab_ref[0, 2 * lev + 1:2 * lev + 2, :]
        xs = jnp.where(chunk_id >= sh, pltpu.roll(x, sh, axis=0), 0.0)
        x = x + xs * a + pltpu.roll(xs, S5_P, axis=1) * bv
        lev += 1
    prev = jnp.where(chunk_id >= 1, pltpu.roll(x, 1, axis=0), 0.0)
    y = y + _dot(prev.astype(BF16), gm_ref[0])
    y = y + d_ref[0] * u
    y_ref[0] = jax.nn.gelu(y)


def _s5_params(lam_re, lam_im, log_dt, b_re, b_im, c_re, c_im, d_skip, nc):
    hp = lax.Precision.HIGHEST
    lam = lax.complex(lam_re.astype(F32), lam_im.astype(F32))
    z = lam * jnp.exp(log_dt.astype(F32))
    lam_bar = jnp.exp(z)
    b_bar = ((lam_bar - 1.0) / lam)[..., None] * lax.complex(b_re.astype(F32), b_im.astype(F32))
    cm = lax.complex(c_re.astype(F32), c_im.astype(F32))
    taus = jnp.arange(S5_L + 1, dtype=F32)
    lp = jnp.exp(z[:, None, :] * taus[None, :, None])
    cl = cm[:, None, :, :] * lp[:, :, None, :]
    kern = (jnp.einsum('gtnp,gpm->gtnm', jnp.real(cl), jnp.real(b_bar), precision=hp)
            - jnp.einsum('gtnp,gpm->gtnm', jnp.imag(cl), jnp.imag(b_bar), precision=hp))
    nblk = S5_L // S5_TB
    kpad = jnp.concatenate(
        [jnp.zeros((S5_G, S5_TB - 1, S5_N, S5_N), F32), kern[:, :S5_L]], axis=1)
    dlt = jnp.arange(nblk)[:, None, None]
    s_i = jnp.arange(S5_TB)[None, :, None]
    t_i = jnp.arange(S5_TB)[None, None, :]
    lag = S5_TB * dlt + t_i - s_i + (S5_TB - 1)
    tt = kpad[:, lag]
    tt = tt.transpose(0, 1, 2, 5, 3, 4).reshape(S5_G, nblk, LANES, LANES)
    tt_rev = tt[:, ::-1].reshape(S5_G, nblk * LANES, LANES)
    hs = lp[:, S5_L - 1 - jnp.arange(S5_L), :, None] * b_bar[:, None, :, :]
    hm = jnp.stack([jnp.real(hs), jnp.imag(hs)], axis=2)
    hm = hm.transpose(0, 1, 4, 2, 3).reshape(S5_G, S5_L * S5_N, 2 * S5_P)
    cg = cl[:, 1:S5_L + 1]
    gm = jnp.stack([jnp.real(cg), -jnp.imag(cg)], axis=1)
    gm = gm.transpose(0, 1, 4, 2, 3).reshape(S5_G, 2 * S5_P, S5_L * S5_N)
    rows = []
    lev = 0
    while (1 << lev) < nc:
        mu = jnp.exp(z * float(S5_L * (1 << lev)))
        a, b = jnp.real(mu), jnp.imag(mu)
        rows.append(jnp.concatenate([a, a], axis=-1))
        rows.append(jnp.concatenate([-b, b], axis=-1))
        lev += 1
    if not rows:
        rows = [jnp.zeros((S5_G, 2 * S5_P), F32)] * 2
    ab = jnp.stack(rows, axis=1)
    dvec = jnp.tile(d_skip.astype(F32).reshape(S5_G, 1, S5_N), (1, S5_L, 1)).reshape(
        S5_G, 1, S5_L * S5_N)
    return tt_rev.astype(BF16), hm.astype(BF16), gm.astype(BF16), ab, dvec


def _s5(u, params, batch, seq):
    tt_rev, hm, gm, ab, dvec = params
    nc = seq // S5_L
    rows = batch * nc
    width = S5_L * S5_N
    ug = u.reshape(batch, nc, S5_L, S5_G, S5_N).transpose(3, 0, 1, 2, 4).reshape(S5_G, rows, width)
    nab = ab.shape[1]
    yg = pl.pallas_call(
        functools.partial(_s5_kernel, nc=nc),
        grid=(S5_G,),
        in_specs=[
            pl.BlockSpec((1, rows, width), lambda g: (g, 0, 0)),
            pl.BlockSpec((1, width, LANES), lambda g: (g, 0, 0)),
            pl.BlockSpec((1, width, 2 * S5_P), lambda g: (g, 0, 0)),
            pl.BlockSpec((1, 2 * S5_P, width), lambda g: (g, 0, 0)),
            pl.BlockSpec((1, nab, 2 * S5_P), lambda g: (g, 0, 0)),
            pl.BlockSpec((1, 1, width), lambda g: (g, 0, 0)),
        ],
        out_specs=pl.BlockSpec((1, rows, width), lambda g: (g, 0, 0)),
        out_shape=jax.ShapeDtypeStruct((S5_G, rows, width), F32),
        compiler_params=pltpu.CompilerParams(
            dimension_semantics=("arbitrary",), vmem_limit_bytes=VMEM_LIMIT),
        name="s5",
    )(ug, tt_rev, hm, gm, ab, dvec)
    return yg.reshape(S5_G, batch, nc, S5_L, S5_N).transpose(1, 2, 3, 0, 4).reshape(
        batch * seq, S5_W)


def _kv_kernel(m_ref, w_ref, k_ref, v_ref):
    kv = _dot(m_ref[...].astype(BF16), w_ref[...])
    k_ref[...] = kv[:, :D_MODEL].astype(BF16)
    v_ref[...] = kv[:, D_MODEL:].astype(BF16)


def _kv_proj(mem2, wkv_bf):
    rows = mem2.shape[0]
    tm = min(256, rows)
    return pl.pallas_call(
        _kv_kernel,
        grid=(rows // tm,),
        in_specs=[pl.BlockSpec((tm, D_MODEL), lambda i: (i, 0)),
                  pl.BlockSpec((D_MODEL, 2 * D_MODEL), lambda i: (0, 0))],
        out_specs=[pl.BlockSpec((tm, D_MODEL), lambda i: (i, 0)),
                   pl.BlockSpec((tm, D_MODEL), lambda i: (i, 0))],
        out_shape=[jax.ShapeDtypeStruct((rows, D_MODEL), BF16),
                   jax.ShapeDtypeStruct((rows, D_MODEL), BF16)],
        compiler_params=pltpu.CompilerParams(
            dimension_semantics=("arbitrary",), vmem_limit_bytes=VMEM_LIMIT),
        name="kv_proj",
    )(mem2, wkv_bf)


def _mid_kernel(oa_ref, y5_ref, h0_ref, k_ref, v_ref, wglu_ref, bglu_ref, wout_ref,
                g1_ref, b1_ref, wq_ref, wo_ref, g2_ref, b2_ref, wr_ref, br_ref,
                h2_ref, idx_ref, gate_ref, rank_ref, cnt_ref, run_ref):
    tq = oa_ref.shape[0]

    @pl.when((pl.program_id(0) == 0) & (pl.program_id(1) == 0))
    def _():
        run_ref[...] = jnp.zeros_like(run_ref)

    y5 = y5_ref[...]
    ob = y5 * jax.nn.sigmoid(_dot(y5.astype(BF16), wglu_ref[...]) + bglu_ref[...])
    mix = _dot(oa_ref[...], wout_ref[:HG_W, :]) + _dot(ob.astype(BF16), wout_ref[HG_W:, :])
    h1 = _layer_norm(ALPHA * h0_ref[...] + mix, g1_ref[...], b1_ref[...])

    q = _dot(h1.astype(BF16), wq_ref[...])
    heads = []
    for hd in range(XA_H):
        cs = slice(hd * XA_D, (hd + 1) * XA_D)
        s = _dot_nt(q[:, cs].astype(BF16), k_ref[0, :, cs]) * XA_SCALE
        e = jnp.exp(s - jnp.max(s, axis=-1, keepdims=True))
        p = e / jnp.sum(e, axis=-1, keepdims=True)
        heads.append(_dot(p.astype(BF16), v_ref[0, :, cs]))
    xa = _dot(jnp.concatenate(heads, axis=1).astype(BF16), wo_ref[...])
    h2 = _layer_norm(ALPHA * h1 + xa, g2_ref[...], b2_ref[...])
    h2_ref[...] = h2

    h_hi, h_mid, h_lo = _split3(h2)
    w_hi, w_mid, w_lo = wr_ref[0], wr_ref[1], wr_ref[2]
    logits = (_dot(h_hi, w_hi) + _dot(h_mid, w_hi) + _dot(h_hi, w_mid)
              + _dot(h_lo, w_hi) + _dot(h_mid, w_mid) + _dot(h_hi, w_lo)) + br_ref[...]
    lane = lax.broadcasted_iota(jnp.int32, (tq, LANES), 1)
    lane_f = lane.astype(F32)
    work = jnp.where(lane < N_EXP, logits, -jnp.inf)
    vals, onehots = [], []
    for kk in range(TOP_K):
        mx = jnp.max(work, axis=-1, keepdims=True)
        sel = jnp.min(jnp.where(work == mx, lane_f, float(LANES)), axis=-1, keepdims=True)
        oh = lane_f == sel
        idx_ref[:, kk:kk + 1] = sel.astype(jnp.int32)
        vals.append(mx)
        onehots.append(oh)
        work = jnp.where(oh, -jnp.inf, work)
    es = [jnp.exp(v - vals[0]) for v in vals]
    den = es[0] + es[1] + es[2] + es[3]
    for kk in range(TOP_K):
        gate_ref[:, kk:kk + 1] = es[kk] / den

    member = jnp.zeros((tq, LANES), F32)
    for oh in onehots:
        member = member + oh.astype(F32)
    r = lax.broadcasted_iota(jnp.int32, (tq, tq), 0)
    c = lax.broadcasted_iota(jnp.int32, (tq, tq), 1)
    before = _dot((c < r).astype(BF16), member.astype(BF16)) + run_ref[...]
    for kk in range(TOP_K):
        rk = jnp.sum(jnp.where(onehots[kk], before, 0.0), axis=-1, keepdims=True)
        rank_ref[:, kk:kk + 1] = rk.astype(jnp.int32)
    run = run_ref[...] + jnp.sum(member, axis=0, keepdims=True)
    run_ref[...] = run
    cnt_ref[...] = run.astype(jnp.int32)


def _mid(oa, y5, h0, kmem, vmem, wglu, bglu, wout, g1, b1, wq, wo, g2, b2, wr3, br, batch, seq):
    t = oa.shape[0]
    tq = min(TQ_MID, seq)
    nj = seq // tq

    def row_spec(w):
        return pl.BlockSpec((tq, w), lambda b, j: (b * nj + j, 0))

    def full_spec(shape):
        nd = len(shape)
        return pl.BlockSpec(shape, lambda b, j: (0,) * nd)

    mem_spec = pl.BlockSpec((1, N_MEM, D_MODEL), lambda b, j: (b, 0, 0))
    return pl.pallas_call(
        _mid_kernel,
        grid=(batch, nj),
        in_specs=[row_spec(HG_W), row_spec(S5_W), row_spec(D_MODEL), mem_spec, mem_spec,
                  full_spec((S5_W, S5_W)), full_spec((1, S5_W)),
                  full_spec((D_MODEL, D_MODEL)), full_spec((1, D_MODEL)), full_spec((1, D_MODEL)),
                  full_spec((D_MODEL, D_MODEL)), full_spec((D_MODEL, D_MODEL)),
                  full_spec((1, D_MODEL)), full_spec((1, D_MODEL)),
                  full_spec((3, D_MODEL, LANES)), full_spec((1, LANES))],
        out_specs=[row_spec(D_MODEL), row_spec(TOP_K), row_spec(TOP_K), row_spec(TOP_K),
                   full_spec((1, LANES))],
        out_shape=[jax.ShapeDtypeStruct((t, D_MODEL), F32),
                   jax.ShapeDtypeStruct((t, TOP_K), jnp.int32),
                   jax.ShapeDtypeStruct((t, TOP_K), F32),
                   jax.ShapeDtypeStruct((t, TOP_K), jnp.int32),
                   jax.ShapeDtypeStruct((1, LANES), jnp.int32)],
        scratch_shapes=[pltpu.VMEM((1, LANES), F32)],
        compiler_params=pltpu.CompilerParams(
            dimension_semantics=("arbitrary", "arbitrary"), vmem_limit_bytes=VMEM_LIMIT),
        name="mid",
    )(oa, y5, h0, kmem, vmem, wglu, bglu, wout, g1, b1, wq, wo, g2, b2, wr3, br)


def _row_copy(src, src_row, dst, dst_row, sem):
    return pltpu.make_async_copy(src.at[pl.ds(src_row, 1)], dst.at[pl.ds(dst_row, 1)], sem)


def _dispatch_kernel(pz_ref, pc_ref, nu_ref, dest_ref, h_ref, xb_ref, zbuf, sem, zsem, *, tm, nb):
    i = pl.program_id(0)
    tk = h_ref.shape[0]
    sub = 8
    nbits = (tm // sub).bit_length() - 1

    def zero_fill(start_not_wait):
        def act(cp):
            if start_not_wait:
                cp.start()
            else:
                cp.wait()

        def per_expert(e, carry):
            cnt = pc_ref[e]
            z0 = pz_ref[e]
            head = (-z0) & (sub - 1)
            for rr in range(sub - 1):
                @pl.when(rr < head)
                def _():
                    act(_row_copy(zbuf, 0, xb_ref, z0 + rr, zsem))
            za = z0 + head
            n_tiles = (cnt - head) >> 3
            for bit in range(nbits):
                size = sub << bit

                @pl.when(((n_tiles >> bit) & 1) == 1)
                def _():
                    off = pl.multiple_of(za + sub * (n_tiles & ((1 << bit) - 1)), sub)
                    act(pltpu.make_async_copy(zbuf.at[pl.ds(0, size)],
                                              xb_ref.at[pl.ds(off, size)], zsem))
            return carry

        lax.fori_loop(0, N_EXP, per_expert, 0)
        for jb in range(N_EXP):
            blk = nu_ref[0] + jb

            @pl.when(blk < nb)
            def _():
                off = pl.multiple_of(blk * tm, tm)
                act(pltpu.make_async_copy(zbuf, xb_ref.at[pl.ds(off, tm)], zsem))

    @pl.when(i == 0)
    def _():
        zbuf[...] = jnp.zeros_like(zbuf)
        zero_fill(True)

    def issue(tk_i, carry):
        for kk in range(TOP_K):
            _row_copy(h_ref, tk_i, xb_ref, dest_ref[tk_i * TOP_K + kk], sem).start()
        return carry

    lax.fori_loop(0, tk, issue, 0, unroll=8)

    def drain(tk_i, carry):
        for kk in range(TOP_K):
            _row_copy(h_ref, 0, xb_ref, 0, sem).wait()
        return carry

    lax.fori_loop(0, tk, drain, 0, unroll=8)

    @pl.when(i == pl.num_programs(0) - 1)
    def _():
        zero_fill(False)


def _dispatch(h2, dest_flat, pad_zero_start, pad_cnt, n_used, tm, nb):
    t = h2.shape[0]
    tk = min(TK_TOK, t)
    grid_spec = pltpu.PrefetchScalarGridSpec(
        num_scalar_prefetch=3,
        grid=(t // tk,),
        in_specs=[pl.BlockSpec((tk * TOP_K,), lambda i, *_: (i,), memory_space=pltpu.SMEM),
                  pl.BlockSpec((tk, D_MODEL), lambda i, *_: (i, 0))],
        out_specs=pl.BlockSpec(memory_space=pl.ANY),
        scratch_shapes=[pltpu.VMEM((tm, D_MODEL), F32),
                        pltpu.SemaphoreType.DMA(()), pltpu.SemaphoreType.DMA(())],
    )
    return pl.pallas_call(
        functools.partial(_dispatch_kernel, tm=tm, nb=nb),
        grid_spec=grid_spec,
        out_shape=jax.ShapeDtypeStruct((nb * tm, D_MODEL), F32),
        compiler_params=pltpu.CompilerParams(
            dimension_semantics=("arbitrary",), vmem_limit_bytes=VMEM_LIMIT,
            has_side_effects=True),
        name="dispatch",
    )(pad_zero_start, pad_cnt, n_used, dest_flat, h2)


def _expert_kernel(be_ref, nu_ref, xb_ref, w1_ref, b1_ref, w2_ref, b2_ref, yb_ref):
    i = pl.program_id(0)

    @pl.when(i < nu_ref[0])
    def _():
        x = xb_ref[...].astype(BF16)
        hh = _dot(x, w1_ref[0].astype(BF16)) + b1_ref[0]
        x_glu = jnp.minimum(hh[:, :D_FF], SWIGLU_LIMIT)
        x_lin = jnp.clip(hh[:, D_FF:], -SWIGLU_LIMIT, SWIGLU_LIMIT)
        act = x_glu * jax.nn.sigmoid(SWIGLU_ALPHA * x_glu) * (x_lin + 1.0)
        yb_ref[...] = _dot(act.astype(BF16), w2_ref[0].astype(BF16)) + b2_ref[0]

    @pl.when(i >= nu_ref[0])
    def _():
        yb_ref[...] = jnp.zeros_like(yb_ref)


def _experts(xb, block_expert, n_used, w1, b1, w2, b2, tm, nb):
    def x_map(i, be, nu):
        return (jnp.minimum(i, nu[0] - 1), 0)

    def w_map(i, be, nu):
        return (be[i], 0, 0)

    grid_spec = pltpu.PrefetchScalarGridSpec(
        num_scalar_prefetch=2,
        grid=(nb,),
        in_specs=[pl.BlockSpec((tm, D_MODEL), x_map),
                  pl.BlockSpec((1, D_MODEL, 2 * D_FF), w_map),
                  pl.BlockSpec((1, 1, 2 * D_FF), w_map),
                  pl.BlockSpec((1, D_FF, D_MODEL), w_map),
                  pl.BlockSpec((1, 1, D_MODEL), w_map)],
        out_specs=pl.BlockSpec((tm, D_MODEL), lambda i, be, nu: (i, 0)),
    )
    return pl.pallas_call(
        _expert_kernel,
        grid_spec=grid_spec,
        out_shape=jax.ShapeDtypeStruct((nb * tm, D_MODEL), F32),
        compiler_params=pltpu.CompilerParams(
            dimension_semantics=("arbitrary",), vmem_limit_bytes=VMEM_LIMIT),
        name="experts",
    )(block_expert, n_used, xb, w1, b1, w2, b2)


def _combine_kernel(dest_ref, h_ref, gate_ref, g_ref, b_ref, yb_ref, o_ref, ybuf, sem):
    tk = h_ref.shape[0]

    def issue(tk_i, carry):
        for kk in range(TOP_K):
            _row_copy(yb_ref, dest_ref[tk_i * TOP_K + kk], ybuf.at[kk], tk_i, sem).start()
        return carry

    lax.fori_loop(0, tk, issue, 0, unroll=8)

    def drain(tk_i, carry):
        for kk in range(TOP_K):
            _row_copy(yb_ref, 0, ybuf.at[kk], 0, sem).wait()
        return carry

    lax.fori_loop(0, tk, drain, 0, unroll=8)

    ff = jnp.zeros((tk, D_MODEL), F32)
    for kk in range(TOP_K):
        ff = ff + gate_ref[:, kk:kk + 1] * ybuf[kk]
    o_ref[...] = _layer_norm(ALPHA * h_ref[...] + ff, g_ref[...], b_ref[...])


def _combine(dest_flat, h2, gates, g3, b3, yb):
    t = h2.shape[0]
    tk = min(TK_TOK, t)
    return pl.pallas_call(
        _combine_kernel,
        grid=(t // tk,),
        in_specs=[pl.BlockSpec((tk * TOP_K,), lambda i: (i,), memory_space=pltpu.SMEM),
                  pl.BlockSpec((tk, D_MODEL), lambda i: (i, 0)),
                  pl.BlockSpec((tk, TOP_K), lambda i: (i, 0)),
                  pl.BlockSpec((1, D_MODEL), lambda i: (0, 0)),
                  pl.BlockSpec((1, D_MODEL), lambda i: (0, 0)),
                  pl.BlockSpec(memory_space=pl.ANY)],
        out_specs=pl.BlockSpec((tk, D_MODEL), lambda i: (i, 0)),
        out_shape=jax.ShapeDtypeStruct((t, D_MODEL), F32),
        scratch_shapes=[pltpu.VMEM((TOP_K, tk, D_MODEL), F32), pltpu.SemaphoreType.DMA(())],
        compiler_params=pltpu.CompilerParams(
            dimension_semantics=("arbitrary",), vmem_limit_bytes=VMEM_LIMIT),
        name="combine",
    )(dest_flat, h2, gates, g3, b3, yb)


def _row(v):
    return v.astype(F32).reshape(1, -1)


def kernel(x, mem, ln_in_g, ln_in_b, w_in, hgrn_lb_logits, hgrn_norm_g, s5_lam_re, s5_lam_im,
           s5_log_dt, s5_b_re, s5_b_im, s5_c_re, s5_c_im, s5_d, s5_w_glu, s5_b_glu, w_out,
           ln1_g, ln1_b, xa_w_q, xa_w_k, xa_w_v, xa_w_o, ln2_g, ln2_b, router_w, router_b,
           exp_w1, exp_b1, exp_w2, exp_b2, ln3_g, ln3_b):
    batch, seq, _ = x.shape
    t = batch * seq
    lb_all = jnp.cumsum(jax.nn.softmax(hgrn_lb_logits.astype(F32), axis=0), axis=0)
    h = x.reshape(t, D_MODEL)
    assert DEPTH == 1 and w_in.shape[0] == 1
    for l in range(DEPTH):
        h0, qfig, u = _ln_inproj(h, _row(ln_in_g), _row(ln_in_b), w_in[l].astype(BF16))
        o_a = _hgrn2(qfig, lb_all[l].reshape(1, HG_W), _row(hgrn_norm_g[l]), batch, seq)
        s5p = _s5_params(s5_lam_re[l], s5_lam_im[l], s5_log_dt[l], s5_b_re[l], s5_b_im[l],
                         s5_c_re[l], s5_c_im[l], s5_d[l], seq // S5_L)
        y5 = _s5(u, s5p, batch, seq)
        wkv = jnp.concatenate([xa_w_k[l], xa_w_v[l]], axis=1).astype(BF16)
        kmem, vmem = _kv_proj(mem.reshape(batch * N_MEM, D_MODEL), wkv)
        kmem = kmem.reshape(batch, N_MEM, D_MODEL)
        vmem = vmem.reshape(batch, N_MEM, D_MODEL)
        wr = jnp.pad(router_w[l].astype(F32), ((0, 0), (0, LANES - N_EXP)))
        wr3 = jnp.stack(_split3(wr), axis=0)
        br = jnp.pad(router_b[l].astype(F32), (0, LANES - N_EXP)).reshape(1, LANES)
        h2, top_idx, gates, rank, cnt = _mid(
            o_a, y5, h0, kmem, vmem, s5_w_glu[l].astype(BF16), _row(s5_b_glu[l]),
            w_out[l].astype(BF16), _row(ln1_g[l]), _row(ln1_b[l]), xa_w_q[l].astype(BF16),
            xa_w_o[l].astype(BF16), _row(ln2_g[l]), _row(ln2_b[l]), wr3, br, batch, seq)

        tm = min(TM_MOE, t)
        nb = (t * TOP_K) // tm + N_EXP
        counts = cnt[0, :N_EXP]
        padded = (counts + tm - 1) // tm * tm
        pad_end = jnp.cumsum(padded)
        pad_start = pad_end - padded
        dest = (pad_start[top_idx] + rank).astype(jnp.int32).reshape(t * TOP_K)
        n_used = (pad_end[-1:] // tm).astype(jnp.int32)
        block_start = jnp.arange(nb, dtype=jnp.int32) * tm
        block_expert = jnp.minimum(
            jnp.searchsorted(pad_end, block_start, side='right'), N_EXP - 1).astype(jnp.int32)
        xb = _dispatch(h2, dest, (pad_start + counts).astype(jnp.int32),
                       (padded - counts).astype(jnp.int32), n_used, tm, nb)
        yb = _experts(xb, block_expert, n_used, exp_w1[l], exp_b1[l].reshape(N_EXP, 1, 2 * D_FF),
                      exp_w2[l], exp_b2[l].reshape(N_EXP, 1, D_MODEL), tm, nb)
        h = _combine(dest, h2, gates, _row(ln3_g[l]), _row(ln3_b[l]), yb)
    return h.reshape(batch, seq, D_MODEL)
```

```python
import functools
import math

import jax
import jax.numpy as jnp
from jax import lax
from jax.experimental import pallas as pl
from jax.experimental.pallas import tpu as pltpu

F32 = jnp.float32
BF16 = jnp.bfloat16

D_MODEL = 1024
HG_W = 512
HG_D = 128
HG_H = HG_W // HG_D
HG_CHUNK = 64
HG_SUB = 16
S5_W = 512
S5_N = 16
S5_G = S5_W // S5_N
S5_P = 64
S5_L = 64
S5_TB = 8
IN_COLS = 4 * HG_W + S5_W
N_MEM = 256
XA_H = 4
XA_D = D_MODEL // XA_H
XA_SCALE = XA_D ** -0.5
N_EXP = 32
TOP_K = 4
D_FF = D_MODEL
SWIGLU_LIMIT = 7.0
SWIGLU_ALPHA = 1.702
DEPTH = 1
ALPHA = (2 * DEPTH) ** 0.25
LN_EPS = 1e-5
RMS_EPS = 1e-6
LANES = 128

TM_IN = 512
TQ_MID = 512
TM_MOE = 256
TK_TOK = 256
VMEM_LIMIT = 56 * 1024 * 1024


def _layer_norm(x, g, b):
    mu = jnp.mean(x, axis=-1, keepdims=True)
    xc = x - mu
    var = jnp.mean(xc * xc, axis=-1, keepdims=True)
    return xc * lax.rsqrt(var + LN_EPS) * g + b


def _dot(a, b):
    return jnp.dot(a, b, preferred_element_type=F32)


def _dot_nt(a, b):
    return lax.dot_general(a, b, (((1,), (1,)), ((), ())), preferred_element_type=F32)


def _dot_tn(a, b):
    return lax.dot_general(a, b, (((0,), (0,)), ((), ())), preferred_element_type=F32)


def _split3(x):
    hi = x.astype(BF16)
    r1 = x - hi.astype(F32)
    mid = r1.astype(BF16)
    lo = (r1 - mid.astype(F32)).astype(BF16)
    return hi, mid, lo


def _ln_inproj_kernel(x_ref, g_ref, b_ref, w_ref, h_ref, qfig_ref, u_ref):
    h = _layer_norm(x_ref[...], g_ref[...], b_ref[...])
    h_ref[...] = h
    p = _dot(h.astype(BF16), w_ref[...])
    qfig_ref[...] = p[:, :4 * HG_W]
    u_ref[...] = p[:, 4 * HG_W:].astype(u_ref.dtype)


def _ln_inproj(x2, g, b, w_bf):
    t = x2.shape[0]
    tm = min(TM_IN, t)
    return pl.pallas_call(
        _ln_inproj_kernel,
        grid=(t // tm,),
        in_specs=[
            pl.BlockSpec((tm, D_MODEL), lambda i: (i, 0)),
            pl.BlockSpec((1, D_MODEL), lambda i: (0, 0)),
            pl.BlockSpec((1, D_MODEL), lambda i: (0, 0)),
            pl.BlockSpec((D_MODEL, IN_COLS), lambda i: (0, 0)),
        ],
        out_specs=[
            pl.BlockSpec((tm, D_MODEL), lambda i: (i, 0)),
            pl.BlockSpec((tm, 4 * HG_W), lambda i: (i, 0)),
            pl.BlockSpec((tm, S5_W), lambda i: (i, 0)),
        ],
        out_shape=[
            jax.ShapeDtypeStruct((t, D_MODEL), F32),
            jax.ShapeDtypeStruct((t, 4 * HG_W), F32),
            jax.ShapeDtypeStruct((t, S5_W), BF16),
        ],
        compiler_params=pltpu.CompilerParams(
            dimension_semantics=("arbitrary",), vmem_limit_bytes=VMEM_LIMIT),
        name="ln_inproj",
    )(x2, g, b, w_bf)


def _hgrn_kernel(q_ref, f_ref, i_ref, g_ref, lb_ref, ng_ref, o_ref, st_ref, k_s, cum_s):
    nseq = q_ref.shape[0]
    rows = nseq * HG_CHUNK

    @pl.when(pl.program_id(0) == 0)
    def _():
        st_ref[...] = jnp.zeros_like(st_ref)

    lb = lb_ref[...]
    f = lb + (1.0 - lb) * jax.nn.sigmoid(f_ref[...].reshape(rows, HG_W))
    k_s[...] = 1.0 - f
    r = lax.broadcasted_iota(jnp.int32, (rows, rows), 0)
    c = lax.broadcasted_iota(jnp.int32, (rows, rows), 1)
    tri = ((c <= r) & ((r // HG_CHUNK) == (c // HG_CHUNK))).astype(BF16)
    hi, mid, lo = _split3(jnp.log2(f))
    cum_s[...] = _dot(tri, hi) + _dot(tri, mid) + _dot(tri, lo)

    row_in_sub = lax.broadcasted_iota(jnp.int32, (HG_SUB, 1), 0)
    ng = ng_ref[...]
    for b in range(nseq):
        r0 = b * HG_CHUNK
        for h in range(HG_H):
            cols = slice(h * HG_D, (h + 1) * HG_D)
            qc = q_ref[b, :, cols]
            kc = k_s[r0:r0 + HG_CHUNK, cols]
            vc = i_ref[b, :, cols]
            cc = cum_s[r0:r0 + HG_CHUNK, cols]
            last = cum_s[r0 + HG_CHUNK - 1:r0 + HG_CHUNK, cols]
            st = st_ref[b, h]
            inter = _dot_nt((qc * jnp.exp2(cc)).astype(BF16), st.astype(BF16))
            vb = vc.astype(BF16)
            outs = []
            for blk in range(HG_CHUNK // HG_SUB):
                rs = slice(HG_SUB * blk, HG_SUB * (blk + 1))
                q_i = qc[rs]
                c_i = cc[rs]
                acc = inter[rs]
                if blk > 0:
                    n = HG_SUB * blk
                    c_ref = cum_s[r0 + n - 1:r0 + n, cols]
                    khat = (kc[:n] * jnp.exp2(c_ref - cc[:n])).astype(BF16)
                    qhat = (q_i * jnp.exp2(c_i - c_ref)).astype(BF16)
                    acc = acc + _dot(_dot_nt(qhat, khat).astype(BF16), vb[:n])
                for jj in range(HG_SUB):
                    srow = HG_SUB * blk + jj
                    c_s = cum_s[r0 + srow:r0 + srow + 1, cols]
                    k_row = k_s[r0 + srow:r0 + srow + 1, cols]
                    v_row = i_ref[b, srow:srow + 1, cols]
                    p = q_i * k_row * jnp.exp2(c_i - c_s)
                    w = jnp.sum(p, axis=-1, keepdims=True)
                    w = jnp.where(row_in_sub >= jj, w, 0.0)
                    acc = acc + w * v_row
                outs.append(acc)
            o = jnp.concatenate(outs, axis=0)
            kd = (kc * jnp.exp2(last - cc)).astype(BF16)
            st_ref[b, h] = st * jnp.exp2(last) + _dot_tn(vb, kd)
            o = o * lax.rsqrt(jnp.mean(o * o, axis=-1, keepdims=True) + RMS_EPS) * ng
            gg = g_ref[b, :, cols]
            o_ref[b, :, cols] = (o * (gg * jax.nn.sigmoid(gg))).astype(o_ref.dtype)


def _hgrn2(qfig, lb, ng, batch, seq):
    nj = seq // HG_CHUNK
    qfig3 = qfig.reshape(batch, seq, 4 * HG_W)

    def col_spec(cb):
        return pl.BlockSpec((batch, HG_CHUNK, HG_W), lambda j: (0, j, cb))

    out = pl.pallas_call(
        _hgrn_kernel,
        grid=(nj,),
        in_specs=[col_spec(0), col_spec(1), col_spec(2), col_spec(3),
                  pl.BlockSpec((1, HG_W), lambda j: (0, 0)),
                  pl.BlockSpec((1, HG_D), lambda j: (0, 0))],
        out_specs=pl.BlockSpec((batch, HG_CHUNK, HG_W), lambda j: (0, j, 0)),
        out_shape=jax.ShapeDtypeStruct((batch, seq, HG_W), BF16),
        scratch_shapes=[pltpu.VMEM((batch, HG_H, HG_D, HG_D), F32),
                        pltpu.VMEM((batch * HG_CHUNK, HG_W), F32),
                        pltpu.VMEM((batch * HG_CHUNK, HG_W), F32)],
        compiler_params=pltpu.CompilerParams(
            dimension_semantics=("arbitrary",), vmem_limit_bytes=VMEM_LIMIT),
        name="hgrn2",
    )(qfig3, qfig3, qfig3, qfig3, lb, ng)
    return out.reshape(batch * seq, HG_W)


def _s5_kernel(u_ref, k2_ref, hm_ref, gm_ref, ab_ref, d_ref, y_ref, *, nc):
    ub = u_ref[0]
    rows = ub.shape[0]
    nblk = S5_L // S5_TB
    blocks = []
    for dlt in reversed(range(nblk)):
        for s in range(S5_TB):
            off = S5_N * (S5_TB * dlt - s + S5_TB - 1)
            base = off // LANES * LANES
            win = k2_ref[0, :, base:base + 2 * LANES]
            if off != base:
                win = pltpu.roll(win, 2 * LANES - (off - base), axis=1)
            blocks.append(win[:, :LANES].astype(BF16))
    tt_rev = jnp.concatenate(blocks, axis=0)
    ys = []
    for bc in range(nblk):
        ys.append(_dot(ub[:, :(bc + 1) * LANES], tt_rev[(nblk - 1 - bc) * LANES:, :]))
    y = jnp.concatenate(ys, axis=1)
    x = _dot(ub, hm_ref[0])
    chunk_id = lax.broadcasted_iota(jnp.int32, (rows, 2 * S5_P), 0) % nc
    lev = 0
    while (1 << lev) < nc:
        sh = 1 << lev
        a = ab_ref[0, 2 * lev:2 * lev + 1, :]
        bv = ab_ref[0, 2 * lev + 1:2 * lev + 2, :]
        xs = jnp.where(chunk_id >= sh, pltpu.roll(x, sh, axis=0), 0.0)
        x = x + xs * a + pltpu.roll(xs, S5_P, axis=1) * bv
        lev += 1
    prev = jnp.where(chunk_id >= 1, pltpu.roll(x, 1, axis=0), 0.0)
    y = y + _dot(prev.astype(BF16), gm_ref[0])
    y = y + d_ref[0] * ub.astype(F32)
    y_ref[0] = jax.nn.gelu(y).astype(y_ref.dtype)


def _s5_tables(lam_re, lam_im, log_dt, b_re, b_im, c_re, c_im, d_skip, nc):
    hp = lax.Precision.HIGHEST
    lam_r, lam_i = lam_re.astype(F32), lam_im.astype(F32)
    dt = jnp.exp(log_dt.astype(F32))
    z_r, z_i = lam_r * dt, lam_i * dt

    def lam_pow(zr, zi, tau):
        mag = jnp.exp(zr * tau)
        return mag * jnp.cos(zi * tau), mag * jnp.sin(zi * tau)

    taus = jnp.arange(S5_L + 1, dtype=F32)[None, :, None]
    p_r, p_i = lam_pow(z_r[:, None, :], z_i[:, None, :], taus)
    x_r, x_i = p_r[:, 1] - 1.0, p_i[:, 1]
    den = lam_r * lam_r + lam_i * lam_i
    f_r = ((x_r * lam_r + x_i * lam_i) / den)[..., None]
    f_i = ((x_i * lam_r - x_r * lam_i) / den)[..., None]
    b_r, b_i = b_re.astype(F32), b_im.astype(F32)
    bb_r = f_r * b_r - f_i * b_i
    bb_i = f_r * b_i + f_i * b_r
    c_r, c_i = c_re.astype(F32)[:, None], c_im.astype(F32)[:, None]
    pk_r, pk_i = p_r[:, :S5_L, None, :], p_i[:, :S5_L, None, :]
    kern = (jnp.einsum('gtnp,gpm->gtnm', c_r * pk_r - c_i * pk_i, bb_r, precision=hp)
            - jnp.einsum('gtnp,gpm->gtnm', c_r * pk_i + c_i * pk_r, bb_i, precision=hp))
    k2 = kern.transpose(0, 3, 1, 2).reshape(S5_G, S5_N, S5_L * S5_N)
    lead = (S5_TB - 1) * S5_N
    k2 = jnp.pad(k2, ((0, 0), (0, 0), (lead, LANES - lead)))
    q_r = p_r[:, S5_L - 1::-1][:, :, None, :]
    q_i = p_i[:, S5_L - 1::-1][:, :, None, :]
    bt_r = bb_r.transpose(0, 2, 1)[:, None]
    bt_i = bb_i.transpose(0, 2, 1)[:, None]
    hm = jnp.concatenate([q_r * bt_r - q_i * bt_i, q_r * bt_i + q_i * bt_r], axis=-1)
    hm = hm.reshape(S5_G, S5_L * S5_N, 2 * S5_P)
    lane = jnp.arange(S5_L * S5_N)
    sel_t = (lane[None, :] // S5_N == jnp.arange(S5_L)[:, None]).astype(F32)
    sel_n = (lane[None, :] % S5_N == jnp.arange(S5_N)[:, None]).astype(F32)
    e_r = jnp.einsum('gtp,tl->gpl', p_r[:, 1:], sel_t, precision=hp)
    e_i = jnp.einsum('gtp,tl->gpl', p_i[:, 1:], sel_t, precision=hp)
    ce_r = jnp.einsum('gnp,nl->gpl', c_re.astype(F32), sel_n, precision=hp)
    ce_i = jnp.einsum('gnp,nl->gpl', c_im.astype(F32), sel_n, precision=hp)
    gm = jnp.concatenate([ce_r * e_r - ce_i * e_i, -(ce_r * e_i + ce_i * e_r)], axis=1)
    rows = []
    lev = 0
    while (1 << lev) < nc:
        a, b = lam_pow(z_r, z_i, float(S5_L * (1 << lev)))
        rows.append(jnp.concatenate([a, a], axis=-1))
        rows.append(jnp.concatenate([-b, b], axis=-1))
        lev += 1
    if not rows:
        rows = [jnp.zeros((S5_G, 2 * S5_P), F32)] * 2
    ab = jnp.stack(rows, axis=1)
    dvec = jnp.einsum('gn,nl->gl', d_skip.astype(F32).reshape(S5_G, S5_N), sel_n, precision=hp)
    return k2, hm.astype(BF16), gm.astype(BF16), ab, dvec.reshape(S5_G, 1, S5_L * S5_N)


def _s5(u, tables, batch, seq):
    k2, hm, gm, ab, dvec = tables
    nc = seq // S5_L
    rows = batch * nc
    width = S5_L * S5_N
    ug = u.reshape(batch, nc, S5_L, S5_G, S5_N).transpose(3, 0, 1, 2, 4).reshape(S5_G, rows, width)
    nab = ab.shape[1]
    yg = pl.pallas_call(
        functools.partial(_s5_kernel, nc=nc),
        grid=(S5_G,),
        in_specs=[
            pl.BlockSpec((1, rows, width), lambda g: (g, 0, 0)),
            pl.BlockSpec((1, S5_N, k2.shape[2]), lambda g: (g, 0, 0)),
            pl.BlockSpec((1, width, 2 * S5_P), lambda g: (g, 0, 0)),
            pl.BlockSpec((1, 2 * S5_P, width), lambda g: (g, 0, 0)),
            pl.BlockSpec((1, nab, 2 * S5_P), lambda g: (g, 0, 0)),
            pl.BlockSpec((1, 1, width), lambda g: (g, 0, 0)),
        ],
        out_specs=pl.BlockSpec((1, rows, width), lambda g: (g, 0, 0)),
        out_shape=jax.ShapeDtypeStruct((S5_G, rows, width), BF16),
        compiler_params=pltpu.CompilerParams(
            dimension_semantics=("arbitrary",), vmem_limit_bytes=VMEM_LIMIT),
        name="s5",
    )(ug, k2, hm, gm, ab, dvec)
    return yg.reshape(S5_G, batch, nc, S5_L, S5_N).transpose(1, 2, 3, 0, 4).reshape(
        batch * seq, S5_W)


def _kv_kernel(m_ref, w_ref, k_ref, v_ref):
    kv = _dot(m_ref[...].astype(BF16), w_ref[...])
    k_ref[...] = kv[:, :D_MODEL].astype(BF16)
    v_ref[...] = kv[:, D_MODEL:].astype(BF16)


def _kv_proj(mem2, wkv_bf):
    rows = mem2.shape[0]
    tm = min(256, rows)
    return pl.pallas_call(
        _kv_kernel,
        grid=(rows // tm,),
        in_specs=[pl.BlockSpec((tm, D_MODEL), lambda i: (i, 0)),
                  pl.BlockSpec((D_MODEL, 2 * D_MODEL), lambda i: (0, 0))],
        out_specs=[pl.BlockSpec((tm, D_MODEL), lambda i: (i, 0)),
                   pl.BlockSpec((tm, D_MODEL), lambda i: (i, 0))],
        out_shape=[jax.ShapeDtypeStruct((rows, D_MODEL), BF16),
                   jax.ShapeDtypeStruct((rows, D_MODEL), BF16)],
        compiler_params=pltpu.CompilerParams(
            dimension_semantics=("arbitrary",), vmem_limit_bytes=VMEM_LIMIT),
        name="kv_proj",
    )(mem2, wkv_bf)


def _mid_kernel(oa_ref, y5_ref, h0_ref, k_ref, v_ref, wglu_ref, bglu_ref, wout_ref,
                g1_ref, b1_ref, wq_ref, wo_ref, g2_ref, b2_ref, wr_ref, br_ref,
                h2_ref, idx_ref, gate_ref, rank_ref, cnt_ref, run_ref):
    tq = oa_ref.shape[0]

    @pl.when((pl.program_id(0) == 0) & (pl.program_id(1) == 0))
    def _():
        run_ref[...] = jnp.zeros_like(run_ref)

    y5b = y5_ref[...]
    ob = y5b.astype(F32) * jax.nn.sigmoid(_dot(y5b, wglu_ref[...]) + bglu_ref[...])
    mix = _dot(oa_ref[...], wout_ref[:HG_W, :]) + _dot(ob.astype(BF16), wout_ref[HG_W:, :])
    h1 = _layer_norm(ALPHA * h0_ref[...] + mix, g1_ref[...], b1_ref[...])

    q = _dot(h1.astype(BF16), wq_ref[...])
    heads = []
    for hd in range(XA_H):
        cs = slice(hd * XA_D, (hd + 1) * XA_D)
        s = _dot_nt(q[:, cs].astype(BF16), k_ref[0, :, cs]) * XA_SCALE
        e = jnp.exp(s - jnp.max(s, axis=-1, keepdims=True))
        p = e / jnp.sum(e, axis=-1, keepdims=True)
        heads.append(_dot(p.astype(BF16), v_ref[0, :, cs]))
    xa = _dot(jnp.concatenate(heads, axis=1).astype(BF16), wo_ref[...])
    h2 = _layer_norm(ALPHA * h1 + xa, g2_ref[...], b2_ref[...])
    h2_ref[...] = h2

    h_hi, h_mid, _ = _split3(h2)
    w_hi, w_mid = wr_ref[0], wr_ref[1]
    logits = (_dot(h_hi, w_hi) + _dot(h_mid, w_hi) + _dot(h_hi, w_mid)) + br_ref[...]
    lane = lax.broadcasted_iota(jnp.int32, (tq, LANES), 1)
    lane_f = lane.astype(F32)
    work = jnp.where(lane < N_EXP, logits, -jnp.inf)
    vals, onehots = [], []
    for kk in range(TOP_K):
        mx = jnp.max(work, axis=-1, keepdims=True)
        sel = jnp.min(jnp.where(work == mx, lane_f, float(LANES)), axis=-1, keepdims=True)
        oh = lane_f == sel
        idx_ref[:, kk:kk + 1] = sel.astype(jnp.int32)
        vals.append(mx)
        onehots.append(oh)
        work = jnp.where(oh, -jnp.inf, work)
    es = [jnp.exp(v - vals[0]) for v in vals]
    den = es[0] + es[1] + es[2] + es[3]
    for kk in range(TOP_K):
        gate_ref[:, kk:kk + 1] = es[kk] / den

    member = jnp.zeros((tq, LANES), F32)
    for oh in onehots:
        member = member + oh.astype(F32)
    r = lax.broadcasted_iota(jnp.int32, (tq, tq), 0)
    c = lax.broadcasted_iota(jnp.int32, (tq, tq), 1)
    before = _dot((c < r).astype(BF16), member.astype(BF16)) + run_ref[...]
    for kk in range(TOP_K):
        rk = jnp.sum(jnp.where(onehots[kk], before, 0.0), axis=-1, keepdims=True)
        rank_ref[:, kk:kk + 1] = rk.astype(jnp.int32)
    run = run_ref[...] + jnp.sum(member, axis=0, keepdims=True)
    run_ref[...] = run
    cnt_ref[...] = run.astype(jnp.int32)


def _mid(oa, y5, h0, kmem, vmem, wglu, bglu, wout, g1, b1, wq, wo, g2, b2, wr3, br, batch, seq):
    t = oa.shape[0]
    tq = min(TQ_MID, seq)
    nj = seq // tq

    def row_spec(w):
        return pl.BlockSpec((tq, w), lambda b, j: (b * nj + j, 0))

    def full_spec(shape):
        nd = len(shape)
        return pl.BlockSpec(shape, lambda b, j: (0,) * nd)

    mem_spec = pl.BlockSpec((1, N_MEM, D_MODEL), lambda b, j: (b, 0, 0))
    return pl.pallas_call(
        _mid_kernel,
        grid=(batch, nj),
        in_specs=[row_spec(HG_W), row_spec(S5_W), row_spec(D_MODEL), mem_spec, mem_spec,
                  full_spec((S5_W, S5_W)), full_spec((1, S5_W)),
                  full_spec((D_MODEL, D_MODEL)), full_spec((1, D_MODEL)), full_spec((1, D_MODEL)),
                  full_spec((D_MODEL, D_MODEL)), full_spec((D_MODEL, D_MODEL)),
                  full_spec((1, D_MODEL)), full_spec((1, D_MODEL)),
                  full_spec((2, D_MODEL, LANES)), full_spec((1, LANES))],
        out_specs=[row_spec(D_MODEL), row_spec(TOP_K), row_spec(TOP_K), row_spec(TOP_K),
                   full_spec((1, LANES))],
        out_shape=[jax.ShapeDtypeStruct((t, D_MODEL), F32),
                   jax.ShapeDtypeStruct((t, TOP_K), jnp.int32),
                   jax.ShapeDtypeStruct((t, TOP_K), F32),
                   jax.ShapeDtypeStruct((t, TOP_K), jnp.int32),
                   jax.ShapeDtypeStruct((1, LANES), jnp.int32)],
        scratch_shapes=[pltpu.VMEM((1, LANES), F32)],
        compiler_params=pltpu.CompilerParams(
            dimension_semantics=("arbitrary", "arbitrary"), vmem_limit_bytes=VMEM_LIMIT),
        name="mid",
    )(oa, y5, h0, kmem, vmem, wglu, bglu, wout, g1, b1, wq, wo, g2, b2, wr3, br)


def _row_copy(src, src_row, dst, dst_row, sem):
    return pltpu.make_async_copy(src.at[pl.ds(src_row, 1)], dst.at[pl.ds(dst_row, 1)], sem)


def _dispatch_kernel(pz_ref, pc_ref, nu_ref, dest_ref, h_ref, xb_ref, zbuf, sem, zsem, *, tm, nb):
    i = pl.program_id(0)
    tk = h_ref.shape[0]
    sub = 8
    nbits = (tm // sub).bit_length() - 1

    def zero_fill(start_not_wait):
        def act(cp):
            if start_not_wait:
                cp.start()
            else:
                cp.wait()

        def per_expert(e, carry):
            cnt = pc_ref[e]
            z0 = pz_ref[e]
            head = (-z0) & (sub - 1)
            for rr in range(sub - 1):
                @pl.when(rr < head)
                def _():
                    act(_row_copy(zbuf, 0, xb_ref, z0 + rr, zsem))
            za = z0 + head
            n_tiles = (cnt - head) >> 3
            for bit in range(nbits):
                size = sub << bit

                @pl.when(((n_tiles >> bit) & 1) == 1)
                def _():
                    off = pl.multiple_of(za + sub * (n_tiles & ((1 << bit) - 1)), sub)
                    act(pltpu.make_async_copy(zbuf.at[pl.ds(0, size)],
                                              xb_ref.at[pl.ds(off, size)], zsem))
            return carry

        lax.fori_loop(0, N_EXP, per_expert, 0)
        for jb in range(N_EXP):
            blk = nu_ref[0] + jb

            @pl.when(blk < nb)
            def _():
                off = pl.multiple_of(blk * tm, tm)
                act(pltpu.make_async_copy(zbuf, xb_ref.at[pl.ds(off, tm)], zsem))

    @pl.when(i == 0)
    def _():
        zbuf[...] = jnp.zeros_like(zbuf)
        zero_fill(True)

    def issue(tk_i, carry):
        for kk in range(TOP_K):
            _row_copy(h_ref, tk_i, xb_ref, dest_ref[tk_i * TOP_K + kk], sem).start(
                priority=kk % 2)
        return carry

    lax.fori_loop(0, tk, issue, 0, unroll=8)

    def drain(tk_i, carry):
        for kk in range(TOP_K):
            _row_copy(h_ref, 0, xb_ref, 0, sem).wait()
        return carry

    lax.fori_loop(0, tk, drain, 0, unroll=8)

    @pl.when(i == pl.num_programs(0) - 1)
    def _():
        zero_fill(False)


def _dispatch(h2, dest_flat, pad_zero_start, pad_cnt, n_used, tm, nb):
    t = h2.shape[0]
    tk = min(TK_TOK, t)
    grid_spec = pltpu.PrefetchScalarGridSpec(
        num_scalar_prefetch=3,
        grid=(t // tk,),
        in_specs=[pl.BlockSpec((tk * TOP_K,), lambda i, *_: (i,), memory_space=pltpu.SMEM),
                  pl.BlockSpec((tk, D_MODEL), lambda i, *_: (i, 0))],
        out_specs=pl.BlockSpec(memory_space=pl.ANY),
        scratch_shapes=[pltpu.VMEM((tm, D_MODEL), F32),
                        pltpu.SemaphoreType.DMA(()), pltpu.SemaphoreType.DMA(())],
    )
    return pl.pallas_call(
        functools.partial(_dispatch_kernel, tm=tm, nb=nb),
        grid_spec=grid_spec,
        out_shape=jax.ShapeDtypeStruct((nb * tm, D_MODEL), F32),
        compiler_params=pltpu.CompilerParams(
            dimension_semantics=("arbitrary",), vmem_limit_bytes=VMEM_LIMIT,
            has_side_effects=True),
        name="dispatch",
    )(pad_zero_start, pad_cnt, n_used, dest_flat, h2)


def _expert_kernel(be_ref, nu_ref, xb_ref, w1_ref, b1_ref, w2_ref, b2_ref, yb_ref, w1b, w2b):
    i = pl.program_id(0)
    used = i < nu_ref[0]
    new_expert = (i == 0) | (be_ref[i] != be_ref[jnp.maximum(i - 1, 0)])

    @pl.when(used & new_expert)
    def _():
        w1b[...] = w1_ref[0].astype(BF16)
        w2b[...] = w2_ref[0].astype(BF16)

    @pl.when(used)
    def _():
        x = xb_ref[...].astype(BF16)
        hh = _dot(x, w1b[...]) + b1_ref[0]
        x_glu = jnp.minimum(hh[:, :D_FF], SWIGLU_LIMIT)
        x_lin = jnp.clip(hh[:, D_FF:], -SWIGLU_LIMIT, SWIGLU_LIMIT)
        act = x_glu * jax.nn.sigmoid(SWIGLU_ALPHA * x_glu) * (x_lin + 1.0)
        yb_ref[...] = _dot(act.astype(BF16), w2b[...]) + b2_ref[0]

    @pl.when(i >= nu_ref[0])
    def _():
        yb_ref[...] = jnp.zeros_like(yb_ref)


def _experts(xb, block_expert, n_used, w1, b1, w2, b2, tm, nb):
    def x_map(i, be, nu):
        return (jnp.minimum(i, nu[0] - 1), 0)

    def w_map(i, be, nu):
        return (be[i], 0, 0)

    grid_spec = pltpu.PrefetchScalarGridSpec(
        num_scalar_prefetch=2,
        grid=(nb,),
        in_specs=[pl.BlockSpec((tm, D_MODEL), x_map),
                  pl.BlockSpec((1, D_MODEL, 2 * D_FF), w_map),
                  pl.BlockSpec((1, 1, 2 * D_FF), w_map),
                  pl.BlockSpec((1, D_FF, D_MODEL), w_map),
                  pl.BlockSpec((1, 1, D_MODEL), w_map)],
        out_specs=pl.BlockSpec((tm, D_MODEL), lambda i, be, nu: (i, 0)),
        scratch_shapes=[pltpu.VMEM((D_MODEL, 2 * D_FF), BF16), pltpu.VMEM((D_FF, D_MODEL), BF16)],
    )
    return pl.pallas_call(
        _expert_kernel,
        grid_spec=grid_spec,
        out_shape=jax.ShapeDtypeStruct((nb * tm, D_MODEL), F32),
        compiler_params=pltpu.CompilerParams(
            dimension_semantics=("arbitrary",), vmem_limit_bytes=VMEM_LIMIT),
        name="experts",
    )(block_expert, n_used, xb, w1, b1, w2, b2)


def _combine_kernel(dest_ref, dnext_ref, h_ref, gate_ref, g_ref, b_ref, yb_ref, o_ref, ybuf, sem):
    i = pl.program_id(0)
    tk = h_ref.shape[0]
    slot = i % 2

    def gather(idx_ref, slot_):
        def issue(tk_i, carry):
            for kk in range(TOP_K):
                _row_copy(yb_ref, idx_ref[tk_i * TOP_K + kk], ybuf.at[slot_, kk], tk_i,
                          sem.at[slot_]).start(priority=kk % 2)
            return carry

        lax.fori_loop(0, tk, issue, 0, unroll=8)

    @pl.when(i == 0)
    def _():
        gather(dest_ref, 0)

    def drain(tk_i, carry):
        for kk in range(TOP_K):
            _row_copy(yb_ref, 0, ybuf.at[slot, kk], 0, sem.at[slot]).wait()
        return carry

    lax.fori_loop(0, tk, drain, 0, unroll=8)

    @pl.when(i + 1 < pl.num_programs(0))
    def _():
        gather(dnext_ref, 1 - slot)

    ff = jnp.zeros((tk, D_MODEL), F32)
    for kk in range(TOP_K):
        ff = ff + gate_ref[:, kk:kk + 1] * ybuf[slot, kk]
    o_ref[...] = _layer_norm(ALPHA * h_ref[...] + ff, g_ref[...], b_ref[...])


def _combine(dest_flat, h2, gates, g3, b3, yb):
    t = h2.shape[0]
    tk = min(TK_TOK, t)
    last = t // tk - 1
    return pl.pallas_call(
        _combine_kernel,
        grid=(t // tk,),
        in_specs=[pl.BlockSpec((tk * TOP_K,), lambda i: (i,), memory_space=pltpu.SMEM),
                  pl.BlockSpec((tk * TOP_K,), lambda i: (jnp.minimum(i + 1, last),),
                               memory_space=pltpu.SMEM),
                  pl.BlockSpec((tk, D_MODEL), lambda i: (i, 0)),
                  pl.BlockSpec((tk, TOP_K), lambda i: (i, 0)),
                  pl.BlockSpec((1, D_MODEL), lambda i: (0, 0)),
                  pl.BlockSpec((1, D_MODEL), lambda i: (0, 0)),
                  pl.BlockSpec(memory_space=pl.ANY)],
        out_specs=pl.BlockSpec((tk, D_MODEL), lambda i: (i, 0)),
        out_shape=jax.ShapeDtypeStruct((t, D_MODEL), F32),
        scratch_shapes=[pltpu.VMEM((2, TOP_K, tk, D_MODEL), F32), pltpu.SemaphoreType.DMA((2,))],
        compiler_params=pltpu.CompilerParams(
            dimension_semantics=("arbitrary",), vmem_limit_bytes=VMEM_LIMIT),
        name="combine",
    )(dest_flat, dest_flat, h2, gates, g3, b3, yb)


def _row(v):
    return v.astype(F32).reshape(1, -1)


def kernel(x, mem, ln_in_g, ln_in_b, w_in, hgrn_lb_logits, hgrn_norm_g, s5_lam_re, s5_lam_im,
           s5_log_dt, s5_b_re, s5_b_im, s5_c_re, s5_c_im, s5_d, s5_w_glu, s5_b_glu, w_out,
           ln1_g, ln1_b, xa_w_q, xa_w_k, xa_w_v, xa_w_o, ln2_g, ln2_b, router_w, router_b,
           exp_w1, exp_b1, exp_w2, exp_b2, ln3_g, ln3_b):
    batch, seq, _ = x.shape
    t = batch * seq
    lb_all = jnp.cumsum(jax.nn.softmax(hgrn_lb_logits.astype(F32), axis=0), axis=0)
    h = x.reshape(t, D_MODEL)
    assert DEPTH == 1 and w_in.shape[0] == 1
    for l in range(DEPTH):
        h0, qfig, u = _ln_inproj(h, _row(ln_in_g), _row(ln_in_b), w_in[l].astype(BF16))
        o_a = _hgrn2(qfig, lb_all[l].reshape(1, HG_W), _row(hgrn_norm_g[l]), batch, seq)
        s5t = _s5_tables(s5_lam_re[l], s5_lam_im[l], s5_log_dt[l], s5_b_re[l], s5_b_im[l],
                         s5_c_re[l], s5_c_im[l], s5_d[l], seq // S5_L)
        y5 = _s5(u, s5t, batch, seq)
        wkv = jnp.concatenate([xa_w_k[l], xa_w_v[l]], axis=1).astype(BF16)
        kmem, vmem = _kv_proj(mem.reshape(batch * N_MEM, D_MODEL), wkv)
        kmem = kmem.reshape(batch, N_MEM, D_MODEL)
        vmem = vmem.reshape(batch, N_MEM, D_MODEL)
        wr = jnp.pad(router_w[l].astype(F32), ((0, 0), (0, LANES - N_EXP)))
        wr3 = jnp.stack(_split3(wr)[:2], axis=0)
        br = jnp.pad(router_b[l].astype(F32), (0, LANES - N_EXP)).reshape(1, LANES)
        h2, top_idx, gates, rank, cnt = _mid(
            o_a, y5, h0, kmem, vmem, s5_w_glu[l].astype(BF16), _row(s5_b_glu[l]),
            w_out[l].astype(BF16), _row(ln1_g[l]), _row(ln1_b[l]), xa_w_q[l].astype(BF16),
            xa_w_o[l].astype(BF16), _row(ln2_g[l]), _row(ln2_b[l]), wr3, br, batch, seq)

        tm = min(TM_MOE, t)
        nb = (t * TOP_K) // tm + N_EXP
        counts = cnt[0, :N_EXP]
        padded = (counts + tm - 1) // tm * tm
        pad_end = jnp.cumsum(padded)
        pad_start = pad_end - padded
        dest = (pad_start[top_idx] + rank).astype(jnp.int32).reshape(t * TOP_K)
        n_used = (pad_end[-1:] // tm).astype(jnp.int32)
        block_start = jnp.arange(nb, dtype=jnp.int32) * tm
        block_expert = jnp.sum(
            pad_end[None, :] <= jnp.minimum(block_start, pad_end[-1] - tm)[:, None],
            axis=1).astype(jnp.int32)
        xb = _dispatch(h2, dest, (pad_start + counts).astype(jnp.int32),
                       (padded - counts).astype(jnp.int32), n_used, tm, nb)
        yb = _experts(xb, block_expert, n_used, exp_w1[l], exp_b1[l].reshape(N_EXP, 1, 2 * D_FF),
                      exp_w2[l], exp_b2[l].reshape(N_EXP, 1, D_MODEL), tm, nb)
        h = _combine(dest, h2, gates, _row(ln3_g[l]), _row(ln3_b[l]), yb)
    return h.reshape(batch, seq, D_MODEL)
```

```python
import functools
import math

import jax
import jax.numpy as jnp
from jax import lax
from jax.experimental import pallas as pl
from jax.experimental.pallas import tpu as pltpu

F32 = jnp.float32
BF16 = jnp.bfloat16

D_MODEL = 1024
HG_W = 512
HG_D = 128
HG_H = HG_W // HG_D
HG_CHUNK = 64
HG_SUB = 16
S5_W = 512
S5_N = 16
S5_G = S5_W // S5_N
S5_P = 64
S5_L = 64
S5_TB = 8
IN_COLS = 4 * HG_W + S5_W
N_MEM = 256
XA_H = 4
XA_D = D_MODEL // XA_H
XA_SCALE = XA_D ** -0.5
N_EXP = 32
TOP_K = 4
D_FF = D_MODEL
SWIGLU_LIMIT = 7.0
SWIGLU_ALPHA = 1.702
DEPTH = 1
ALPHA = (2 * DEPTH) ** 0.25
LN_EPS = 1e-5
RMS_EPS = 1e-6
LANES = 128

TM_IN = 512
TQ_MID = 512
TM_MOE = 256
TK_TOK = 256
VMEM_LIMIT = 56 * 1024 * 1024


def _layer_norm(x, g, b):
    mu = jnp.mean(x, axis=-1, keepdims=True)
    xc = x - mu
    var = jnp.mean(xc * xc, axis=-1, keepdims=True)
    return xc * lax.rsqrt(var + LN_EPS) * g + b


def _dot(a, b):
    return jnp.dot(a, b, preferred_element_type=F32)


def _dot_nt(a, b):
    return lax.dot_general(a, b, (((1,), (1,)), ((), ())), preferred_element_type=F32)


def _dot_tn(a, b):
    return lax.dot_general(a, b, (((0,), (0,)), ((), ())), preferred_element_type=F32)


def _split3(x):
    hi = x.astype(BF16)
    r1 = x - hi.astype(F32)
    mid = r1.astype(BF16)
    lo = (r1 - mid.astype(F32)).astype(BF16)
    return hi, mid, lo


def _ln_inproj_kernel(x_ref, g_ref, b_ref, w_ref, h_ref, qfig_ref, u_ref):
    h = _layer_norm(x_ref[...], g_ref[...], b_ref[...])
    h_ref[...] = h
    p = _dot(h.astype(BF16), w_ref[...])
    qfig_ref[...] = p[:, :4 * HG_W]
    u_ref[...] = p[:, 4 * HG_W:].astype(u_ref.dtype)


def _ln_inproj(x2, g, b, w_bf):
    t = x2.shape[0]
    tm = min(TM_IN, t)
    return pl.pallas_call(
        _ln_inproj_kernel,
        grid=(t // tm,),
        in_specs=[
            pl.BlockSpec((tm, D_MODEL), lambda i: (i, 0)),
            pl.BlockSpec((1, D_MODEL), lambda i: (0, 0)),
            pl.BlockSpec((1, D_MODEL), lambda i: (0, 0)),
            pl.BlockSpec((D_MODEL, IN_COLS), lambda i: (0, 0)),
        ],
        out_specs=[
            pl.BlockSpec((tm, D_MODEL), lambda i: (i, 0)),
            pl.BlockSpec((tm, 4 * HG_W), lambda i: (i, 0)),
            pl.BlockSpec((tm, S5_W), lambda i: (i, 0)),
        ],
        out_shape=[
            jax.ShapeDtypeStruct((t, D_MODEL), F32),
            jax.ShapeDtypeStruct((t, 4 * HG_W), F32),
            jax.ShapeDtypeStruct((t, S5_W), BF16),
        ],
        compiler_params=pltpu.CompilerParams(
            dimension_semantics=("arbitrary",), vmem_limit_bytes=VMEM_LIMIT),
        name="ln_inproj",
    )(x2, g, b, w_bf)


def _hgrn_kernel(q_ref, f_ref, i_ref, g_ref, lb_ref, ng_ref, o_ref, st_ref, k_s, cum_s):
    nseq = q_ref.shape[0]
    rows = nseq * HG_CHUNK

    @pl.when(pl.program_id(0) == 0)
    def _():
        st_ref[...] = jnp.zeros_like(st_ref)

    lb = lb_ref[...]
    f = lb + (1.0 - lb) * jax.nn.sigmoid(f_ref[...].reshape(rows, HG_W))
    k_s[...] = 1.0 - f
    r = lax.broadcasted_iota(jnp.int32, (rows, rows), 0)
    c = lax.broadcasted_iota(jnp.int32, (rows, rows), 1)
    tri = ((c <= r) & ((r // HG_CHUNK) == (c // HG_CHUNK))).astype(BF16)
    hi, mid, lo = _split3(jnp.log2(f))
    cum_s[...] = _dot(tri, hi) + _dot(tri, mid) + _dot(tri, lo)

    row_in_sub = lax.broadcasted_iota(jnp.int32, (HG_SUB, 1), 0)
    ng = ng_ref[...]
    for b in range(nseq):
        r0 = b * HG_CHUNK
        for h in range(HG_H):
            cols = slice(h * HG_D, (h + 1) * HG_D)
            qc = q_ref[b, :, cols]
            kc = k_s[r0:r0 + HG_CHUNK, cols]
            vc = i_ref[b, :, cols]
            cc = cum_s[r0:r0 + HG_CHUNK, cols]
            last = cum_s[r0 + HG_CHUNK - 1:r0 + HG_CHUNK, cols]
            st = st_ref[b, h]
            inter = _dot_nt((qc * jnp.exp2(cc)).astype(BF16), st.astype(BF16))
            vb = vc.astype(BF16)
            outs = []
            for blk in range(HG_CHUNK // HG_SUB):
                rs = slice(HG_SUB * blk, HG_SUB * (blk + 1))
                q_i = qc[rs]
                c_i = cc[rs]
                acc = inter[rs]
                if blk > 0:
                    n = HG_SUB * blk
                    c_ref = cum_s[r0 + n - 1:r0 + n, cols]
                    khat = (kc[:n] * jnp.exp2(c_ref - cc[:n])).astype(BF16)
                    qhat = (q_i * jnp.exp2(c_i - c_ref)).astype(BF16)
                    acc = acc + _dot(_dot_nt(qhat, khat).astype(BF16), vb[:n])
                for jj in range(HG_SUB):
                    srow = HG_SUB * blk + jj
                    c_s = cum_s[r0 + srow:r0 + srow + 1, cols]
                    k_row = k_s[r0 + srow:r0 + srow + 1, cols]
                    v_row = i_ref[b, srow:srow + 1, cols]
                    p = q_i * k_row * jnp.exp2(c_i - c_s)
                    w = jnp.sum(p, axis=-1, keepdims=True)
                    w = jnp.where(row_in_sub >= jj, w, 0.0)
                    acc = acc + w * v_row
                outs.append(acc)
            o = jnp.concatenate(outs, axis=0)
            kd = (kc * jnp.exp2(last - cc)).astype(BF16)
            st_ref[b, h] = st * jnp.exp2(last) + _dot_tn(vb, kd)
            o = o * lax.rsqrt(jnp.mean(o * o, axis=-1, keepdims=True) + RMS_EPS) * ng
            gg = g_ref[b, :, cols]
            o_ref[b, :, cols] = (o * (gg * jax.nn.sigmoid(gg))).astype(o_ref.dtype)


def _hgrn2(qfig, lb, ng, batch, seq):
    nj = seq // HG_CHUNK
    qfig3 = qfig.reshape(batch, seq, 4 * HG_W)

    def col_spec(cb):
        return pl.BlockSpec((batch, HG_CHUNK, HG_W), lambda j: (0, j, cb))

    out = pl.pallas_call(
        _hgrn_kernel,
        grid=(nj,),
        in_specs=[col_spec(0), col_spec(1), col_spec(2), col_spec(3),
                  pl.BlockSpec((1, HG_W), lambda j: (0, 0)),
                  pl.BlockSpec((1, HG_D), lambda j: (0, 0))],
        out_specs=pl.BlockSpec((batch, HG_CHUNK, HG_W), lambda j: (0, j, 0)),
        out_shape=jax.ShapeDtypeStruct((batch, seq, HG_W), BF16),
        scratch_shapes=[pltpu.VMEM((batch, HG_H, HG_D, HG_D), F32),
                        pltpu.VMEM((batch * HG_CHUNK, HG_W), F32),
                        pltpu.VMEM((batch * HG_CHUNK, HG_W), F32)],
        compiler_params=pltpu.CompilerParams(
            dimension_semantics=("arbitrary",), vmem_limit_bytes=VMEM_LIMIT),
        name="hgrn2",
    )(qfig3, qfig3, qfig3, qfig3, lb, ng)
    return out.reshape(batch * seq, HG_W)


def _s5_kernel(u_ref, k2_ref, hm_ref, gm_ref, ab_ref, d_ref, y_ref, *, nc):
    ub = u_ref[0]
    rows = ub.shape[0]
    nblk = S5_L // S5_TB
    blocks = []
    for dlt in reversed(range(nblk)):
        for s in range(S5_TB):
            off = S5_N * (S5_TB * dlt - s + S5_TB - 1)
            base = off // LANES * LANES
            win = k2_ref[0, :, base:base + 2 * LANES]
            if off != base:
                win = pltpu.roll(win, 2 * LANES - (off - base), axis=1)
            blocks.append(win[:, :LANES].astype(BF16))
    tt_rev = jnp.concatenate(blocks, axis=0)
    ys = []
    for bc in range(nblk):
        ys.append(_dot(ub[:, :(bc + 1) * LANES], tt_rev[(nblk - 1 - bc) * LANES:, :]))
    y = jnp.concatenate(ys, axis=1)
    x = _dot(ub, hm_ref[0])
    chunk_id = lax.broadcasted_iota(jnp.int32, (rows, 2 * S5_P), 0) % nc
    lev = 0
    while (1 << lev) < nc:
        sh = 1 << lev
        a = ab_ref[0, 2 * lev:2 * lev + 1, :]
        bv = ab_ref[0, 2 * lev + 1:2 * lev + 2, :]
        xs = jnp.where(chunk_id >= sh, pltpu.roll(x, sh, axis=0), 0.0)
        x = x + xs * a + pltpu.roll(xs, S5_P, axis=1) * bv
        lev += 1
    prev = jnp.where(chunk_id >= 1, pltpu.roll(x, 1, axis=0), 0.0)
    y = y + _dot(prev.astype(BF16), gm_ref[0])
    y = y + d_ref[0] * ub.astype(F32)
    y_ref[0] = jax.nn.gelu(y).astype(y_ref.dtype)


def _s5_tables(lam_re, lam_im, log_dt, b_re, b_im, c_re, c_im, d_skip, nc):
    hp = lax.Precision.HIGHEST
    lam_r, lam_i = lam_re.astype(F32), lam_im.astype(F32)
    dt = jnp.exp(log_dt.astype(F32))
    z_r, z_i = lam_r * dt, lam_i * dt

    def lam_pow(zr, zi, tau):
        mag = jnp.exp(zr * tau)
        return mag * jnp.cos(zi * tau), mag * jnp.sin(zi * tau)

    taus = jnp.arange(S5_L + 1, dtype=F32)[None, :, None]
    p_r, p_i = lam_pow(z_r[:, None, :], z_i[:, None, :], taus)
    x_r, x_i = p_r[:, 1] - 1.0, p_i[:, 1]
    den = lam_r * lam_r + lam_i * lam_i
    f_r = ((x_r * lam_r + x_i * lam_i) / den)[..., None]
    f_i = ((x_i * lam_r - x_r * lam_i) / den)[..., None]
    b_r, b_i = b_re.astype(F32), b_im.astype(F32)
    bb_r = f_r * b_r - f_i * b_i
    bb_i = f_r * b_i + f_i * b_r
    lane = jnp.arange(S5_L * S5_N)
    sel_t = (lane[None, :] // S5_N == jnp.arange(S5_L)[:, None]).astype(F32)
    sel_n = (lane[None, :] % S5_N == jnp.arange(S5_N)[:, None]).astype(F32)
    ce_r = jnp.einsum('gnp,nl->gpl', c_re.astype(F32), sel_n, precision=hp)
    ce_i = jnp.einsum('gnp,nl->gpl', c_im.astype(F32), sel_n, precision=hp)
    t_r = jnp.einsum('gtp,tl->gpl', p_r[:, :S5_L], sel_t, precision=hp)
    t_i = jnp.einsum('gtp,tl->gpl', p_i[:, :S5_L], sel_t, precision=hp)
    k2 = (jnp.einsum('gpm,gpl->gml', bb_r, ce_r * t_r - ce_i * t_i, precision=hp)
          - jnp.einsum('gpm,gpl->gml', bb_i, ce_r * t_i + ce_i * t_r, precision=hp))
    lead = (S5_TB - 1) * S5_N
    k2 = jnp.pad(k2, ((0, 0), (0, 0), (lead, LANES - lead)))
    q_r = p_r[:, S5_L - 1::-1][:, :, None, :]
    q_i = p_i[:, S5_L - 1::-1][:, :, None, :]
    bt_r = bb_r.transpose(0, 2, 1)[:, None]
    bt_i = bb_i.transpose(0, 2, 1)[:, None]
    hm = jnp.concatenate([q_r * bt_r - q_i * bt_i, q_r * bt_i + q_i * bt_r], axis=-1)
    hm = hm.reshape(S5_G, S5_L * S5_N, 2 * S5_P)
    e_r = jnp.einsum('gtp,tl->gpl', p_r[:, 1:], sel_t, precision=hp)
    e_i = jnp.einsum('gtp,tl->gpl', p_i[:, 1:], sel_t, precision=hp)
    gm =jnp.concatenate([ce_r * e_r - ce_i * e_i, -(ce_r * e_i + ce_i * e_r)], axis=1)
    rows = []
    lev = 0
    while (1 << lev) < nc:
        a, b = lam_pow(z_r, z_i, float(S5_L * (1 << lev)))
        rows.append(jnp.concatenate([a, a], axis=-1))
        rows.append(jnp.concatenate([-b, b], axis=-1))
        lev += 1
    if not rows:
        rows = [jnp.zeros((S5_G, 2 * S5_P), F32)] * 2
    ab = jnp.stack(rows, axis=1)
    dvec = jnp.einsum('gn,nl->gl', d_skip.astype(F32).reshape(S5_G, S5_N), sel_n, precision=hp)
    return k2, hm.astype(BF16), gm.astype(BF16), ab, dvec.reshape(S5_G, 1, S5_L * S5_N)


def _s5(u, tables, batch, seq):
    k2, hm, gm, ab, dvec = tables
    nc = seq // S5_L
    rows = batch * nc
    width = S5_L * S5_N
    ug = u.reshape(batch, nc, S5_L, S5_G, S5_N).transpose(3, 0, 1, 2, 4).reshape(S5_G, rows, width)
    nab = ab.shape[1]
    yg = pl.pallas_call(
        functools.partial(_s5_kernel, nc=nc),
        grid=(S5_G,),
        in_specs=[
            pl.BlockSpec((1, rows, width), lambda g: (g, 0, 0)),
            pl.BlockSpec((1, S5_N, k2.shape[2]), lambda g: (g, 0, 0)),
            pl.BlockSpec((1, width, 2 * S5_P), lambda g: (g, 0, 0)),
            pl.BlockSpec((1, 2 * S5_P, width), lambda g: (g, 0, 0)),
            pl.BlockSpec((1, nab, 2 * S5_P), lambda g: (g, 0, 0)),
            pl.BlockSpec((1, 1, width), lambda g: (g, 0, 0)),
        ],
        out_specs=pl.BlockSpec((1, rows, width), lambda g: (g, 0, 0)),
        out_shape=jax.ShapeDtypeStruct((S5_G, rows, width), BF16),
        compiler_params=pltpu.CompilerParams(
            dimension_semantics=("arbitrary",), vmem_limit_bytes=VMEM_LIMIT),
        name="s5",
    )(ug, k2, hm, gm, ab, dvec)
    return yg.reshape(S5_G, batch, nc, S5_L, S5_N).transpose(1, 2, 3, 0, 4).reshape(
        batch * seq, S5_W)


def _kv_kernel(m_ref, w_ref, k_ref, v_ref):
    kv = _dot(m_ref[...].astype(BF16), w_ref[...])
    k_ref[...] = kv[:, :D_MODEL].astype(BF16)
    v_ref[...] = kv[:, D_MODEL:].astype(BF16)


def _kv_proj(mem2, wkv_bf):
    rows = mem2.shape[0]
    tm = min(256, rows)
    return pl.pallas_call(
        _kv_kernel,
        grid=(rows // tm,),
        in_specs=[pl.BlockSpec((tm, D_MODEL), lambda i: (i, 0)),
                  pl.BlockSpec((D_MODEL, 2 * D_MODEL), lambda i: (0, 0))],
        out_specs=[pl.BlockSpec((tm, D_MODEL), lambda i: (i, 0)),
                   pl.BlockSpec((tm, D_MODEL), lambda i: (i, 0))],
        out_shape=[jax.ShapeDtypeStruct((rows, D_MODEL), BF16),
                   jax.ShapeDtypeStruct((rows, D_MODEL), BF16)],
        compiler_params=pltpu.CompilerParams(
            dimension_semantics=("arbitrary",), vmem_limit_bytes=VMEM_LIMIT),
        name="kv_proj",
    )(mem2, wkv_bf)


def _mid_kernel(oa_ref, y5_ref, h0_ref, k_ref, v_ref, wglu_ref, bglu_ref, wout_ref,
                g1_ref, b1_ref, wq_ref, wo_ref, g2_ref, b2_ref, wr_ref, br_ref,
                h2_ref, idx_ref, gate_ref, rank_ref, cnt_ref, run_ref):
    tq = oa_ref.shape[0]

    @pl.when((pl.program_id(0) == 0) & (pl.program_id(1) == 0))
    def _():
        run_ref[...] = jnp.zeros_like(run_ref)

    y5b = y5_ref[...]
    ob = y5b.astype(F32) * jax.nn.sigmoid(_dot(y5b, wglu_ref[...]) + bglu_ref[...])
    mix = _dot(oa_ref[...], wout_ref[:HG_W, :]) + _dot(ob.astype(BF16), wout_ref[HG_W:, :])
    h1 = _layer_norm(ALPHA * h0_ref[...] + mix, g1_ref[...], b1_ref[...])

    q = _dot(h1.astype(BF16), wq_ref[...])
    heads = []
    for hd in range(XA_H):
        cs = slice(hd * XA_D, (hd + 1) * XA_D)
        s = _dot_nt(q[:, cs].astype(BF16), k_ref[0, :, cs]) * XA_SCALE
        e = jnp.exp(s - jnp.max(s, axis=-1, keepdims=True))
        p = e / jnp.sum(e, axis=-1, keepdims=True)
        heads.append(_dot(p.astype(BF16), v_ref[0, :, cs]))
    xa = _dot(jnp.concatenate(heads, axis=1).astype(BF16), wo_ref[...])
    h2 = _layer_norm(ALPHA * h1 + xa, g2_ref[...], b2_ref[...])
    h2_ref[...] = h2

    h_hi, h_mid, _ = _split3(h2)
    w_hi, w_mid = wr_ref[0], wr_ref[1]
    logits = (_dot(h_hi, w_hi) + _dot(h_mid, w_hi) + _dot(h_hi, w_mid)) + br_ref[...]
    lane = lax.broadcasted_iota(jnp.int32, (tq, LANES), 1)
    lane_f = lane.astype(F32)
    work = jnp.where(lane < N_EXP, logits, -jnp.inf)
    vals, onehots = [], []
    for kk in range(TOP_K):
        mx = jnp.max(work, axis=-1, keepdims=True)
        sel = jnp.min(jnp.where(work == mx, lane_f, float(LANES)), axis=-1, keepdims=True)
        oh = lane_f == sel
        idx_ref[:, kk:kk + 1] = sel.astype(jnp.int32)
        vals.append(mx)
        onehots.append(oh)
        work = jnp.where(oh, -jnp.inf, work)
    es = [jnp.exp(v - vals[0]) for v in vals]
    den = es[0] + es[1] + es[2] + es[3]
    for kk in range(TOP_K):
        gate_ref[:, kk:kk + 1] = es[kk] / den

    member = jnp.zeros((tq, LANES), F32)
    for oh in onehots:
        member = member + oh.astype(F32)
    r = lax.broadcasted_iota(jnp.int32, (tq, tq), 0)
    c = lax.broadcasted_iota(jnp.int32, (tq, tq), 1)
    before = _dot((c < r).astype(BF16), member.astype(BF16)) + run_ref[...]
    for kk in range(TOP_K):
        rk = jnp.sum(jnp.where(onehots[kk], before, 0.0), axis=-1, keepdims=True)
        rank_ref[:, kk:kk + 1] = rk.astype(jnp.int32)
    run = run_ref[...] + jnp.sum(member, axis=0, keepdims=True)
    run_ref[...] = run
    cnt_ref[...] = run.astype(jnp.int32)


def _mid(oa, y5, h0, kmem, vmem, wglu, bglu, wout, g1, b1, wq, wo, g2, b2, wr3, br, batch, seq):
    t = oa.shape[0]
    tq = min(TQ_MID, seq)
    nj = seq // tq

    def row_spec(w):
        return pl.BlockSpec((tq, w), lambda b, j: (b * nj + j, 0))

    def full_spec(shape):
        nd = len(shape)
        return pl.BlockSpec(shape, lambda b, j: (0,) * nd)

    mem_spec = pl.BlockSpec((1, N_MEM, D_MODEL), lambda b, j: (b, 0, 0))
    return pl.pallas_call(
        _mid_kernel,
        grid=(batch, nj),
        in_specs=[row_spec(HG_W), row_spec(S5_W), row_spec(D_MODEL), mem_spec, mem_spec,
                  full_spec((S5_W, S5_W)), full_spec((1, S5_W)),
                  full_spec((D_MODEL, D_MODEL)), full_spec((1, D_MODEL)), full_spec((1, D_MODEL)),
                  full_spec((D_MODEL, D_MODEL)), full_spec((D_MODEL, D_MODEL)),
                  full_spec((1, D_MODEL)), full_spec((1, D_MODEL)),
                  full_spec((2, D_MODEL, LANES)), full_spec((1, LANES))],
        out_specs=[row_spec(D_MODEL), row_spec(TOP_K), row_spec(TOP_K), row_spec(TOP_K),
                   full_spec((1, LANES))],
        out_shape=[jax.ShapeDtypeStruct((t, D_MODEL), F32),
                   jax.ShapeDtypeStruct((t, TOP_K), jnp.int32),
                   jax.ShapeDtypeStruct((t, TOP_K), F32),
                   jax.ShapeDtypeStruct((t, TOP_K), jnp.int32),
                   jax.ShapeDtypeStruct((1, LANES), jnp.int32)],
        scratch_shapes=[pltpu.VMEM((1, LANES), F32)],
        compiler_params=pltpu.CompilerParams(
            dimension_semantics=("arbitrary", "arbitrary"), vmem_limit_bytes=VMEM_LIMIT),
        name="mid",
    )(oa, y5, h0, kmem, vmem, wglu, bglu, wout, g1, b1, wq, wo, g2, b2, wr3, br)


def _row_copy(src, src_row, dst, dst_row, sem):
    return pltpu.make_async_copy(src.at[pl.ds(src_row, 1)], dst.at[pl.ds(dst_row, 1)], sem)


def _expert_kernel(be_ref, slot_ref, nxt_ref, nu_ref, rt_ref, rtn_ref, h_ref, w1_ref, b1_ref,
                   w2_ref, b2_ref, yb_ref, xbuf, w1f, w2f, w1b, w2b, gsem, sem1, sem2):
    i = pl.program_id(0)
    tm = yb_ref.shape[0]
    used = i < nu_ref[0]
    e = be_ref[i]
    slot = slot_ref[i]
    xs = i % 2
    new_expert = (i == 0) | (e != be_ref[jnp.maximum(i - 1, 0)])

    def gather(idx_ref, xs_):
        def issue(j, carry):
            for u in range(8):
                r = j * 8 + u
                _row_copy(h_ref, idx_ref[r], xbuf.at[xs_], r, gsem.at[xs_]).start(priority=u % 2)
            return carry

        lax.fori_loop(0, tm // 8, issue, 0)

    def fetch(expert, slot_):
        return (pltpu.make_async_copy(w1_ref.at[expert], w1f.at[slot_], sem1.at[slot_]),
                pltpu.make_async_copy(w2_ref.at[expert], w2f.at[slot_], sem2.at[slot_]))

    @pl.when(used & (i == 0))
    def _():
        gather(rt_ref, 0)
        for cp in fetch(e, slot):
            cp.start()

    @pl.when(i + 1 < nu_ref[0])
    def _():
        gather(rtn_ref, 1 - xs)

    @pl.when(used & new_expert)
    def _():
        for cp in fetch(e, slot):
            cp.wait()
        w1b[...] = w1f[slot].astype(BF16)
        w2b[...] = w2f[slot].astype(BF16)
        nxt = nxt_ref[i]

        @pl.when(nxt >= 0)
        def _():
            for cp in fetch(nxt, 1 - slot):
                cp.start()

    @pl.when(used)
    def _():
        def drain(j, carry):
            for u in range(8):
                _row_copy(h_ref, 0, xbuf.at[xs], 0, gsem.at[xs]).wait()
            return carry

        lax.fori_loop(0, tm // 8, drain, 0)
        x = xbuf[xs].astype(BF16)
        hh = _dot(x, w1b[...]) + b1_ref[0]
        x_glu = jnp.minimum(hh[:, :D_FF], SWIGLU_LIMIT)
        x_lin = jnp.clip(hh[:, D_FF:], -SWIGLU_LIMIT, SWIGLU_LIMIT)
        act = x_glu * jax.nn.sigmoid(SWIGLU_ALPHA * x_glu) * (x_lin + 1.0)
        yb_ref[...] = _dot(act.astype(BF16), w2b[...]) + b2_ref[0]

    @pl.when(i >= nu_ref[0])
    def _():
        yb_ref[...] = jnp.zeros_like(yb_ref)


def _experts(h2, row_tok, block_expert, block_slot, block_next, n_used, w1, b1, w2, b2, tm, nb):
    def rt_map(i, be, sl, nx, nu):
        return (jnp.minimum(i, nu[0] - 1),)

    def rtn_map(i, be, sl, nx, nu):
        return (jnp.minimum(i + 1, nu[0] - 1),)

    def b_map(i, be, sl, nx, nu):
        return (be[i], 0, 0)

    grid_spec = pltpu.PrefetchScalarGridSpec(
        num_scalar_prefetch=4,
        grid=(nb,),
        in_specs=[pl.BlockSpec((tm,), rt_map, memory_space=pltpu.SMEM),
                  pl.BlockSpec((tm,), rtn_map, memory_space=pltpu.SMEM),
                  pl.BlockSpec(memory_space=pl.ANY),
                  pl.BlockSpec(memory_space=pl.ANY),
                  pl.BlockSpec((1, 1, 2 * D_FF), b_map),
                  pl.BlockSpec(memory_space=pl.ANY),
                  pl.BlockSpec((1, 1, D_MODEL), b_map)],
        out_specs=pl.BlockSpec((tm, D_MODEL), lambda i, be, sl, nx, nu: (i, 0)),
        scratch_shapes=[pltpu.VMEM((2, tm, D_MODEL), F32),
                        pltpu.VMEM((2, D_MODEL, 2 * D_FF), F32),
                        pltpu.VMEM((2, D_FF, D_MODEL), F32),
                        pltpu.VMEM((D_MODEL, 2 * D_FF), BF16),
                        pltpu.VMEM((D_FF, D_MODEL), BF16),
                        pltpu.SemaphoreType.DMA((2,)),
                        pltpu.SemaphoreType.DMA((2,)), pltpu.SemaphoreType.DMA((2,))],
    )
    return pl.pallas_call(
        _expert_kernel,
        grid_spec=grid_spec,
        out_shape=jax.ShapeDtypeStruct((nb * tm, D_MODEL), F32),
        compiler_params=pltpu.CompilerParams(
            dimension_semantics=("arbitrary",), vmem_limit_bytes=VMEM_LIMIT),
        name="experts",
    )(block_expert, block_slot, block_next, n_used, row_tok, row_tok, h2, w1, b1, w2, b2)


def _combine_kernel(dest_ref, dnext_ref, h_ref, gate_ref, g_ref, b_ref, yb_ref, o_ref, ybuf, sem):
    i = pl.program_id(0)
    tk = h_ref.shape[0]
    slot = i % 2

    def gather(idx_ref, slot_):
        def issue(tk_i, carry):
            for kk in range(TOP_K):
                _row_copy(yb_ref, idx_ref[tk_i * TOP_K + kk], ybuf.at[slot_, kk], tk_i,
                          sem.at[slot_]).start(priority=kk % 2)
            return carry

        lax.fori_loop(0, tk, issue, 0, unroll=8)

    @pl.when(i == 0)
    def _():
        gather(dest_ref, 0)

    def drain(tk_i, carry):
        for kk in range(TOP_K):
            _row_copy(yb_ref, 0, ybuf.at[slot, kk], 0, sem.at[slot]).wait()
        return carry

    lax.fori_loop(0, tk, drain, 0, unroll=8)

    @pl.when(i + 1 < pl.num_programs(0))
    def _():
        gather(dnext_ref, 1 - slot)

    ff = jnp.zeros((tk, D_MODEL), F32)
    for kk in range(TOP_K):
        ff = ff + gate_ref[:, kk:kk + 1] * ybuf[slot, kk]
    o_ref[...] = _layer_norm(ALPHA * h_ref[...] + ff, g_ref[...], b_ref[...])


def _combine(dest_flat, h2, gates, g3, b3, yb):
    t = h2.shape[0]
    tk = min(TK_TOK, t)
    last = t // tk - 1
    return pl.pallas_call(
        _combine_kernel,
        grid=(t // tk,),
        in_specs=[pl.BlockSpec((tk * TOP_K,), lambda i: (i,), memory_space=pltpu.SMEM),
                  pl.BlockSpec((tk * TOP_K,), lambda i: (jnp.minimum(i + 1, last),),
                               memory_space=pltpu.SMEM),
                  pl.BlockSpec((tk, D_MODEL), lambda i: (i, 0)),
                  pl.BlockSpec((tk, TOP_K), lambda i: (i, 0)),
                  pl.BlockSpec((1, D_MODEL), lambda i: (0, 0)),
                  pl.BlockSpec((1, D_MODEL), lambda i: (0, 0)),
                  pl.BlockSpec(memory_space=pl.ANY)],
        out_specs=pl.BlockSpec((tk, D_MODEL), lambda i: (i, 0)),
        out_shape=jax.ShapeDtypeStruct((t, D_MODEL), F32),
        scratch_shapes=[pltpu.VMEM((2, TOP_K, tk, D_MODEL), F32), pltpu.SemaphoreType.DMA((2,))],
        compiler_params=pltpu.CompilerParams(
            dimension_semantics=("arbitrary",), vmem_limit_bytes=VMEM_LIMIT),
        name="combine",
    )(dest_flat, dest_flat, h2, gates, g3, b3, yb)


def _row(v):
    return v.astype(F32).reshape(1, -1)


def kernel(x, mem, ln_in_g, ln_in_b, w_in, hgrn_lb_logits, hgrn_norm_g, s5_lam_re, s5_lam_im,
           s5_log_dt, s5_b_re, s5_b_im, s5_c_re, s5_c_im, s5_d, s5_w_glu, s5_b_glu, w_out,
           ln1_g, ln1_b, xa_w_q, xa_w_k, xa_w_v, xa_w_o, ln2_g, ln2_b, router_w, router_b,
           exp_w1, exp_b1, exp_w2, exp_b2, ln3_g, ln3_b):
    batch, seq, _ = x.shape
    t = batch * seq
    lb_all = jnp.cumsum(jax.nn.softmax(hgrn_lb_logits.astype(F32), axis=0), axis=0)
    h = x.reshape(t, D_MODEL)
    assert DEPTH == 1 and w_in.shape[0] == 1
    for l in range(DEPTH):
        h0, qfig, u = _ln_inproj(h, _row(ln_in_g), _row(ln_in_b), w_in[l].astype(BF16))
        o_a = _hgrn2(qfig, lb_all[l].reshape(1, HG_W), _row(hgrn_norm_g[l]), batch, seq)
        s5t = _s5_tables(s5_lam_re[l], s5_lam_im[l], s5_log_dt[l], s5_b_re[l], s5_b_im[l],
                         s5_c_re[l], s5_c_im[l], s5_d[l], seq // S5_L)
        y5 = _s5(u, s5t, batch, seq)
        wkv = jnp.concatenate([xa_w_k[l], xa_w_v[l]], axis=1).astype(BF16)
        kmem, vmem = _kv_proj(mem.reshape(batch * N_MEM, D_MODEL), wkv)
        kmem = kmem.reshape(batch, N_MEM, D_MODEL)
        vmem = vmem.reshape(batch, N_MEM, D_MODEL)
        wr = jnp.pad(router_w[l].astype(F32), ((0, 0), (0, LANES - N_EXP)))
        wr3 = jnp.stack(_split3(wr)[:2], axis=0)
        br = jnp.pad(router_b[l].astype(F32), (0, LANES - N_EXP)).reshape(1, LANES)
        h2, top_idx, gates, rank, cnt = _mid(
            o_a, y5, h0, kmem, vmem, s5_w_glu[l].astype(BF16), _row(s5_b_glu[l]),
            w_out[l].astype(BF16), _row(ln1_g[l]), _row(ln1_b[l]), xa_w_q[l].astype(BF16),
            xa_w_o[l].astype(BF16), _row(ln2_g[l]), _row(ln2_b[l]), wr3, br, batch, seq)

        tm = min(TM_MOE, t)
        nb = (t * TOP_K) // tm + N_EXP
        counts = cnt[0, :N_EXP]
        padded = (counts + tm - 1) // tm * tm
        pad_end = jnp.cumsum(padded)
        pad_start = pad_end - padded
        dest = (pad_start[top_idx] + rank).astype(jnp.int32).reshape(t * TOP_K)
        n_used = (pad_end[-1:] // tm).astype(jnp.int32)
        block_start = jnp.arange(nb, dtype=jnp.int32) * tm
        block_expert = jnp.sum(
            pad_end[None, :] <= jnp.minimum(block_start, pad_end[-1] - tm)[:, None],
            axis=1).astype(jnp.int32)
        nonempty = counts > 0
        order = jnp.cumsum(nonempty.astype(jnp.int32)) - 1
        eid = jnp.arange(N_EXP, dtype=jnp.int32)
        later = jnp.where(nonempty[None, :] & (eid[None, :] > eid[:, None]), eid[None, :], N_EXP)
        succ = jnp.min(later, axis=1)
        succ = jnp.where(succ < N_EXP, succ, -1).astype(jnp.int32)
        block_slot = (order[block_expert] % 2).astype(jnp.int32)
        block_next = succ[block_expert]
        row_tok = jnp.zeros((nb * tm,), jnp.int32).at[dest].set(
            jnp.arange(t * TOP_K, dtype=jnp.int32) // TOP_K)
        yb = _experts(h2, row_tok, block_expert, block_slot, block_next, n_used, exp_w1[l],
                      exp_b1[l].reshape(N_EXP, 1, 2 * D_FF), exp_w2[l],
                      exp_b2[l].reshape(N_EXP, 1, D_MODEL), tm, nb)
        h = _combine(dest, h2, gates, _row(ln3_g[l]), _row(ln3_b[l]), yb)
    return h.reshape(batch, seq, D_MODEL)
```

```python
import functools
import math

import jax
import jax.numpy as jnp
from jax import lax
from jax.experimental import pallas as pl
from jax.experimental.pallas import tpu as pltpu

F32 = jnp.float32
BF16 = jnp.bfloat16

D_MODEL = 1024
HG_W = 512
HG_D = 128
HG_H = HG_W // HG_D
HG_CHUNK = 64
HG_SUB = 16
S5_W = 512
S5_N = 16
S5_G = S5_W // S5_N
S5_P = 64
S5_L = 64
S5_TB = 8
IN_COLS = 4 * HG_W + S5_W
N_MEM = 256
XA_H = 4
XA_D = D_MODEL // XA_H
XA_SCALE = XA_D ** -0.5
N_EXP = 32
TOP_K = 4
D_FF = D_MODEL
SWIGLU_LIMIT = 7.0
SWIGLU_ALPHA = 1.702
DEPTH = 1
ALPHA = (2 * DEPTH) ** 0.25
LN_EPS = 1e-5
RMS_EPS = 1e-6
LANES = 128

TM_IN = 512
TQ_MID = 512
TM_MOE = 256
TK_TOK = 256
VMEM_LIMIT = 56 * 1024 * 1024


def _layer_norm(x, g, b):
    mu = jnp.mean(x, axis=-1, keepdims=True)
    xc = x - mu
    var = jnp.mean(xc * xc, axis=-1, keepdims=True)
    return xc * lax.rsqrt(var + LN_EPS) * g + b


def _dot(a, b):
    return jnp.dot(a, b, preferred_element_type=F32)


def _dot_nt(a, b):
    return lax.dot_general(a, b, (((1,), (1,)), ((), ())), preferred_element_type=F32)


def _dot_tn(a, b):
    return lax.dot_general(a, b, (((0,), (0,)), ((), ())), preferred_element_type=F32)


def _split3(x):
    hi = x.astype(BF16)
    r1 = x - hi.astype(F32)
    mid = r1.astype(BF16)
    lo = (r1 - mid.astype(F32)).astype(BF16)
    return hi, mid, lo


def _ln_inproj_kernel(x_ref, g_ref, b_ref, w_ref, h_ref, qfig_ref, u_ref):
    h = _layer_norm(x_ref[...], g_ref[...], b_ref[...])
    h_ref[...] = h
    p = _dot(h.astype(BF16), w_ref[...])
    qfig_ref[...] = p[:, :4 * HG_W]
    u_ref[...] = p[:, 4 * HG_W:].astype(u_ref.dtype)


def _ln_inproj(x2, g, b, w_bf):
    t = x2.shape[0]
    tm = min(TM_IN, t)
    return pl.pallas_call(
        _ln_inproj_kernel,
        grid=(t // tm,),
        in_specs=[
            pl.BlockSpec((tm, D_MODEL), lambda i: (i, 0)),
            pl.BlockSpec((1, D_MODEL), lambda i: (0, 0)),
            pl.BlockSpec((1, D_MODEL), lambda i: (0, 0)),
            pl.BlockSpec((D_MODEL, IN_COLS), lambda i: (0, 0)),
        ],
        out_specs=[
            pl.BlockSpec((tm, D_MODEL), lambda i: (i, 0)),
            pl.BlockSpec((tm, 4 * HG_W), lambda i: (i, 0)),
            pl.BlockSpec((tm, S5_W), lambda i: (i, 0)),
        ],
        out_shape=[
            jax.ShapeDtypeStruct((t, D_MODEL), F32),
            jax.ShapeDtypeStruct((t, 4 * HG_W), F32),
            jax.ShapeDtypeStruct((t, S5_W), BF16),
        ],
        compiler_params=pltpu.CompilerParams(
            dimension_semantics=("arbitrary",), vmem_limit_bytes=VMEM_LIMIT),
        name="ln_inproj",
    )(x2, g, b, w_bf)


def _hgrn_kernel(q_ref, f_ref, i_ref, g_ref, lb_ref, ng_ref, o_ref, st_ref, k_s, cum_s):
    nseq = q_ref.shape[0]
    rows = nseq * HG_CHUNK

    @pl.when(pl.program_id(0) == 0)
    def _():
        st_ref[...] = jnp.zeros_like(st_ref)

    lb = lb_ref[...]
    f = lb + (1.0 - lb) * jax.nn.sigmoid(f_ref[...].reshape(rows, HG_W))
    k_s[...] = 1.0 - f
    r = lax.broadcasted_iota(jnp.int32, (rows, rows), 0)
    c = lax.broadcasted_iota(jnp.int32, (rows, rows), 1)
    tri = ((c <= r) & ((r // HG_CHUNK) == (c // HG_CHUNK))).astype(BF16)
    hi, mid, lo = _split3(jnp.log2(f))
    cum_s[...] = _dot(tri, hi) + _dot(tri, mid) + _dot(tri, lo)

    row_in_sub = lax.broadcasted_iota(jnp.int32, (HG_SUB, 1), 0)
    ng = ng_ref[...]
    for b in range(nseq):
        r0 = b * HG_CHUNK
        for h in range(HG_H):
            cols = slice(h * HG_D, (h + 1) * HG_D)
            qc = q_ref[b, :, cols]
            kc = k_s[r0:r0 + HG_CHUNK, cols]
            vc = i_ref[b, :, cols]
            cc = cum_s[r0:r0 + HG_CHUNK, cols]
            last = cum_s[r0 + HG_CHUNK - 1:r0 + HG_CHUNK, cols]
            st = st_ref[b, h]
            inter = _dot_nt((qc * jnp.exp2(cc)).astype(BF16), st.astype(BF16))
            vb = vc.astype(BF16)
            outs = []
            for blk in range(HG_CHUNK // HG_SUB):
                rs = slice(HG_SUB * blk, HG_SUB * (blk + 1))
                q_i = qc[rs]
                c_i = cc[rs]
                acc = inter[rs]
                if blk > 0:
                    n = HG_SUB * blk
                    c_ref = cum_s[r0 + n - 1:r0 + n, cols]
                    khat = (kc[:n] * jnp.exp2(c_ref - cc[:n])).astype(BF16)
                    qhat = (q_i * jnp.exp2(c_i - c_ref)).astype(BF16)
                    acc = acc + _dot(_dot_nt(qhat, khat).astype(BF16), vb[:n])
                for jj in range(HG_SUB):
                    srow = HG_SUB * blk + jj
                    c_s = cum_s[r0 + srow:r0 + srow + 1, cols]
                    k_row = k_s[r0 + srow:r0 + srow + 1, cols]
                    v_row = i_ref[b, srow:srow + 1, cols]
                    p = q_i * k_row * jnp.exp2(c_i - c_s)
                    w = jnp.sum(p, axis=-1, keepdims=True)
                    w = jnp.where(row_in_sub >= jj, w, 0.0)
                    acc = acc + w * v_row
                outs.append(acc)
            o = jnp.concatenate(outs, axis=0)
            kd = (kc * jnp.exp2(last - cc)).astype(BF16)
            st_ref[b, h] = st * jnp.exp2(last) + _dot_tn(vb, kd)
            o = o * lax.rsqrt(jnp.mean(o * o, axis=-1, keepdims=True) + RMS_EPS) * ng
            gg = g_ref[b, :, cols]
            o_ref[b, :, cols] = (o * (gg * jax.nn.sigmoid(gg))).astype(o_ref.dtype)


def _hgrn2(qfig, lb, ng, batch, seq):
    nj = seq // HG_CHUNK
    qfig3 = qfig.reshape(batch, seq, 4 * HG_W)

    def col_spec(cb):
        return pl.BlockSpec((batch, HG_CHUNK, HG_W), lambda j: (0, j, cb))

    out = pl.pallas_call(
        _hgrn_kernel,
        grid=(nj,),
        in_specs=[col_spec(0), col_spec(1), col_spec(2), col_spec(3),
                  pl.BlockSpec((1, HG_W), lambda j: (0, 0)),
                  pl.BlockSpec((1, HG_D), lambda j: (0, 0))],
        out_specs=pl.BlockSpec((batch, HG_CHUNK, HG_W), lambda j: (0, j, 0)),
        out_shape=jax.ShapeDtypeStruct((batch, seq, HG_W), BF16),
        scratch_shapes=[pltpu.VMEM((batch, HG_H, HG_D, HG_D), F32),
                        pltpu.VMEM((batch * HG_CHUNK, HG_W), F32),
                        pltpu.VMEM((batch * HG_CHUNK, HG_W), F32)],
        compiler_params=pltpu.CompilerParams(
            dimension_semantics=("arbitrary",), vmem_limit_bytes=VMEM_LIMIT),
        name="hgrn2",
    )(qfig3, qfig3, qfig3, qfig3, lb, ng)
    return out.reshape(batch * seq, HG_W)


def _s5_kernel(u_ref, k2_ref, hm_ref, gm_ref, ab_ref, d_ref, y_ref, *, nc):
    ub = u_ref[0]
    rows = ub.shape[0]
    nblk = S5_L // S5_TB
    blocks = []
    for dlt in reversed(range(nblk)):
        for s in range(S5_TB):
            off = S5_N * (S5_TB * dlt - s + S5_TB - 1)
            base = off // LANES * LANES
            win = k2_ref[0, :, base:base + 2 * LANES]
            if off != base:
                win = pltpu.roll(win, 2 * LANES - (off - base), axis=1)
            blocks.append(win[:, :LANES].astype(BF16))
    tt_rev = jnp.concatenate(blocks, axis=0)
    ys = []
    for bc in range(nblk):
        ys.append(_dot(ub[:, :(bc + 1) * LANES], tt_rev[(nblk - 1 - bc) * LANES:, :]))
    y = jnp.concatenate(ys, axis=1)
    x = _dot(ub, hm_ref[0])
    chunk_id = lax.broadcasted_iota(jnp.int32, (rows, 2 * S5_P), 0) % nc
    lev = 0
    while (1 << lev) < nc:
        sh = 1 << lev
        a = ab_ref[0, 2 * lev:2 * lev + 1, :]
        bv = ab_ref[0, 2 * lev + 1:2 * lev + 2, :]
        xs = jnp.where(chunk_id >= sh, pltpu.roll(x, sh, axis=0), 0.0)
        x = x + xs * a + pltpu.roll(xs, S5_P, axis=1) * bv
        lev += 1
    prev = jnp.where(chunk_id >= 1, pltpu.roll(x, 1, axis=0), 0.0)
    y = y + _dot(prev.astype(BF16), gm_ref[0])
    y = y + d_ref[0] * ub.astype(F32)
    y_ref[0] = jax.nn.gelu(y).astype(y_ref.dtype)


def _s5_tables(lam_re, lam_im, log_dt, b_re, b_im, c_re, c_im, d_skip, nc):
    hp = lax.Precision.HIGHEST
    lam_r, lam_i = lam_re.astype(F32), lam_im.astype(F32)
    dt = jnp.exp(log_dt.astype(F32))
    z_r, z_i = lam_r * dt, lam_i * dt

    def lam_pow(zr, zi, tau):
        mag = jnp.exp(zr * tau)
        return mag * jnp.cos(zi * tau), mag * jnp.sin(zi * tau)

    taus = jnp.arange(S5_L + 1, dtype=F32)[None, :, None]
    p_r, p_i = lam_pow(z_r[:, None, :], z_i[:, None, :], taus)
    x_r, x_i = p_r[:, 1] - 1.0, p_i[:, 1]
    den = lam_r * lam_r + lam_i * lam_i
    f_r = ((x_r * lam_r + x_i * lam_i) / den)[..., None]
    f_i = ((x_i * lam_r - x_r * lam_i) / den)[..., None]
    b_r, b_i = b_re.astype(F32), b_im.astype(F32)
    bb_r = f_r * b_r - f_i * b_i
    bb_i = f_r * b_i + f_i * b_r
    lane = jnp.arange(S5_L * S5_N)
    sel_t = (lane[None, :] // S5_N == jnp.arange(S5_L)[:, None]).astype(F32)
    sel_n = (lane[None, :] % S5_N == jnp.arange(S5_N)[:, None]).astype(F32)
    ce_r = jnp.einsum('gnp,nl->gpl', c_re.astype(F32), sel_n, precision=hp)
    ce_i = jnp.einsum('gnp,nl->gpl', c_im.astype(F32), sel_n, precision=hp)
    t_r = jnp.einsum('gtp,tl->gpl', p_r[:, :S5_L], sel_t, precision=hp)
    t_i = jnp.einsum('gtp,tl->gpl', p_i[:, :S5_L], sel_t, precision=hp)
    k2 = (jnp.einsum('gpm,gpl->gml', bb_r, ce_r * t_r - ce_i * t_i, precision=hp)
          - jnp.einsum('gpm,gpl->gml', bb_i, ce_r * t_i + ce_i * t_r, precision=hp))
    lead = (S5_TB - 1) * S5_N
    k2 = jnp.pad(k2, ((0, 0), (0, 0), (lead, LANES - lead)))
    q_r = p_r[:, S5_L - 1::-1][:, :, None, :]
    q_i = p_i[:, S5_L - 1::-1][:, :, None, :]
    bt_r = bb_r.transpose(0, 2, 1)[:, None]
    bt_i = bb_i.transpose(0, 2, 1)[:, None]
    hm = jnp.concatenate([q_r * bt_r - q_i * bt_i, q_r * bt_i + q_i * bt_r], axis=-1)
    hm = hm.reshape(S5_G, S5_L * S5_N, 2 * S5_P)
    e_r = jnp.einsum('gtp,tl->gpl', p_r[:, 1:], sel_t, precision=hp)
    e_i = jnp.einsum('gtp,tl->gpl', p_i[:, 1:], sel_t, precision=hp)
    gm =jnp.concatenate([ce_r * e_r - ce_i * e_i, -(ce_r * e_i + ce_i * e_r)], axis=1)
    rows = []
    lev = 0
    while (1 << lev) < nc:
        a, b = lam_pow(z_r, z_i, float(S5_L * (1 << lev)))
        rows.append(jnp.concatenate([a, a], axis=-1))
        rows.append(jnp.concatenate([-b, b], axis=-1))
        lev += 1
    if not rows:
        rows = [jnp.zeros((S5_G, 2 * S5_P), F32)] * 2
    ab = jnp.stack(rows, axis=1)
    dvec = jnp.einsum('gn,nl->gl', d_skip.astype(F32).reshape(S5_G, S5_N), sel_n, precision=hp)
    return k2, hm.astype(BF16), gm.astype(BF16), ab, dvec.reshape(S5_G, 1, S5_L * S5_N)


def _s5(u, tables, batch, seq):
    k2, hm, gm, ab, dvec = tables
    nc = seq // S5_L
    rows = batch * nc
    width = S5_L * S5_N
    ug = u.reshape(batch, nc, S5_L, S5_G, S5_N).transpose(3, 0, 1, 2, 4).reshape(S5_G, rows, width)
    nab = ab.shape[1]
    yg = pl.pallas_call(
        functools.partial(_s5_kernel, nc=nc),
        grid=(S5_G,),
        in_specs=[
            pl.BlockSpec((1, rows, width), lambda g: (g, 0, 0)),
            pl.BlockSpec((1, S5_N, k2.shape[2]), lambda g: (g, 0, 0)),
            pl.BlockSpec((1, width, 2 * S5_P), lambda g: (g, 0, 0)),
            pl.BlockSpec((1, 2 * S5_P, width), lambda g: (g, 0, 0)),
            pl.BlockSpec((1, nab, 2 * S5_P), lambda g: (g, 0, 0)),
            pl.BlockSpec((1, 1, width), lambda g: (g, 0, 0)),
        ],
        out_specs=pl.BlockSpec((1, rows, width), lambda g: (g, 0, 0)),
        out_shape=jax.ShapeDtypeStruct((S5_G, rows, width), BF16),
        compiler_params=pltpu.CompilerParams(
            dimension_semantics=("arbitrary",), vmem_limit_bytes=VMEM_LIMIT),
        name="s5",
    )(ug, k2, hm, gm, ab, dvec)
    return yg.reshape(S5_G, batch, nc, S5_L, S5_N).transpose(1, 2, 3, 0, 4).reshape(
        batch * seq, S5_W)


def _kv_kernel(m_ref, w_ref, k_ref, v_ref):
    kv = _dot(m_ref[...].astype(BF16), w_ref[...])
    k_ref[...] = kv[:, :D_MODEL].astype(BF16)
    v_ref[...] = kv[:, D_MODEL:].astype(BF16)


def _kv_proj(mem2, wkv_bf):
    rows = mem2.shape[0]
    tm = min(256, rows)
    return pl.pallas_call(
        _kv_kernel,
        grid=(rows // tm,),
        in_specs=[pl.BlockSpec((tm, D_MODEL), lambda i: (i, 0)),
                  pl.BlockSpec((D_MODEL, 2 * D_MODEL), lambda i: (0, 0))],
        out_specs=[pl.BlockSpec((tm, D_MODEL), lambda i: (i, 0)),
                   pl.BlockSpec((tm, D_MODEL), lambda i: (i, 0))],
        out_shape=[jax.ShapeDtypeStruct((rows, D_MODEL), BF16),
                   jax.ShapeDtypeStruct((rows, D_MODEL), BF16)],
        compiler_params=pltpu.CompilerParams(
            dimension_semantics=("arbitrary",), vmem_limit_bytes=VMEM_LIMIT),
        name="kv_proj",
    )(mem2, wkv_bf)


def _mid_kernel(oa_ref, y5_ref, h0_ref, k_ref, v_ref, wglu_ref, bglu_ref, wout_ref,
                g1_ref, b1_ref, wq_ref, wo_ref, g2_ref, b2_ref, wr_ref, br_ref,
                h2_ref, idx_ref, gate_ref, rank_ref, cnt_ref, run_ref):
    tq = oa_ref.shape[0]

    @pl.when((pl.program_id(0) == 0) & (pl.program_id(1) == 0))
    def _():
        run_ref[...] = jnp.zeros_like(run_ref)

    y5b = y5_ref[...]
    ob = y5b.astype(F32) * jax.nn.sigmoid(_dot(y5b, wglu_ref[...]) + bglu_ref[...])
    mix = _dot(oa_ref[...], wout_ref[:HG_W, :]) + _dot(ob.astype(BF16), wout_ref[HG_W:, :])
    h1 = _layer_norm(ALPHA * h0_ref[...] + mix, g1_ref[...], b1_ref[...])

    q = _dot(h1.astype(BF16), wq_ref[...])
    heads = []
    for hd in range(XA_H):
        cs = slice(hd * XA_D, (hd + 1) * XA_D)
        s = _dot_nt(q[:, cs].astype(BF16), k_ref[0, :, cs]) * XA_SCALE
        e = jnp.exp(s - jnp.max(s, axis=-1, keepdims=True))
        p = e / jnp.sum(e, axis=-1, keepdims=True)
        heads.append(_dot(p.astype(BF16), v_ref[0, :, cs]))
    xa = _dot(jnp.concatenate(heads, axis=1).astype(BF16), wo_ref[...])
    h2 = _layer_norm(ALPHA * h1 + xa, g2_ref[...], b2_ref[...])
    h2_ref[...] = h2

    h_hi, h_mid, _ = _split3(h2)
    w_hi, w_mid = wr_ref[0], wr_ref[1]
    logits = (_dot(h_hi, w_hi) + _dot(h_mid, w_hi) + _dot(h_hi, w_mid)) + br_ref[...]
    lane = lax.broadcasted_iota(jnp.int32, (tq, LANES), 1)
    lane_f = lane.astype(F32)
    work = jnp.where(lane < N_EXP, logits, -jnp.inf)
    vals, onehots = [], []
    for kk in range(TOP_K):
        mx = jnp.max(work, axis=-1, keepdims=True)
        sel = jnp.min(jnp.where(work == mx, lane_f, float(LANES)), axis=-1, keepdims=True)
        oh = lane_f == sel
        idx_ref[:, kk:kk + 1] = sel.astype(jnp.int32)
        vals.append(mx)
        onehots.append(oh)
        work = jnp.where(oh, -jnp.inf, work)
    es = [jnp.exp(v - vals[0]) for v in vals]
    den = es[0] + es[1] + es[2] + es[3]
    for kk in range(TOP_K):
        gate_ref[:, kk:kk + 1] = es[kk] / den

    member = jnp.zeros((tq, LANES), F32)
    for oh in onehots:
        member = member + oh.astype(F32)
    r = lax.broadcasted_iota(jnp.int32, (tq, tq), 0)
    c = lax.broadcasted_iota(jnp.int32, (tq, tq), 1)
    before = _dot((c < r).astype(BF16), member.astype(BF16)) + run_ref[...]
    for kk in range(TOP_K):
        rk = jnp.sum(jnp.where(onehots[kk], before, 0.0), axis=-1, keepdims=True)
        rank_ref[:, kk:kk + 1] = rk.astype(jnp.int32)
    run = run_ref[...] + jnp.sum(member, axis=0, keepdims=True)
    run_ref[...] = run
    cnt_ref[...] = run.astype(jnp.int32)


def _mid(oa, y5, h0, kmem, vmem, wglu, bglu, wout, g1, b1, wq, wo, g2, b2, wr3, br, batch, seq):
    t = oa.shape[0]
    tq = min(TQ_MID, seq)
    nj = seq // tq

    def row_spec(w):
        return pl.BlockSpec((tq, w), lambda b, j: (b * nj + j, 0))

    def full_spec(shape):
        nd = len(shape)
        return pl.BlockSpec(shape, lambda b, j: (0,) * nd)

    mem_spec = pl.BlockSpec((1, N_MEM, D_MODEL), lambda b, j: (b, 0, 0))
    return pl.pallas_call(
        _mid_kernel,
        grid=(batch, nj),
        in_specs=[row_spec(HG_W), row_spec(S5_W), row_spec(D_MODEL), mem_spec, mem_spec,
                  full_spec((S5_W, S5_W)), full_spec((1, S5_W)),
                  full_spec((D_MODEL, D_MODEL)), full_spec((1, D_MODEL)), full_spec((1, D_MODEL)),
                  full_spec((D_MODEL, D_MODEL)), full_spec((D_MODEL, D_MODEL)),
                  full_spec((1, D_MODEL)), full_spec((1, D_MODEL)),
                  full_spec((2, D_MODEL, LANES)), full_spec((1, LANES))],
        out_specs=[row_spec(D_MODEL), row_spec(TOP_K), row_spec(TOP_K), row_spec(TOP_K),
                   full_spec((1, LANES))],
        out_shape=[jax.ShapeDtypeStruct((t, D_MODEL), F32),
                   jax.ShapeDtypeStruct((t, TOP_K), jnp.int32),
                   jax.ShapeDtypeStruct((t, TOP_K), F32),
                   jax.ShapeDtypeStruct((t, TOP_K), jnp.int32),
                   jax.ShapeDtypeStruct((1, LANES), jnp.int32)],
        scratch_shapes=[pltpu.VMEM((1, LANES), F32)],
        compiler_params=pltpu.CompilerParams(
            dimension_semantics=("arbitrary", "arbitrary"), vmem_limit_bytes=VMEM_LIMIT),
        name="mid",
    )(oa, y5, h0, kmem, vmem, wglu, bglu, wout, g1, b1, wq, wo, g2, b2, wr3, br)


def _row_copy(src, src_row, dst, dst_row, sem):
    return pltpu.make_async_copy(src.at[pl.ds(src_row, 1)], dst.at[pl.ds(dst_row, 1)], sem)


def _dispatch_kernel(pz_ref, pc_ref, nu_ref, dest_ref, h_ref, xb_ref, zbuf, sem, zsem, *, tm, nb):
    i = pl.program_id(0)
    tk = h_ref.shape[0]
    sub = 8
    nbits = (tm // sub).bit_length() - 1

    def zero_fill(start_not_wait):
        def act(cp):
            if start_not_wait:
                cp.start()
            else:
                cp.wait()

        def per_expert(e, carry):
            cnt = pc_ref[e]
            z0 = pz_ref[e]
            head = (-z0) & (sub - 1)
            for rr in range(sub - 1):
                @pl.when(rr < head)
                def _():
                    act(_row_copy(zbuf, 0, xb_ref, z0 + rr, zsem))
            za = z0 + head
            n_tiles = (cnt - head) >> 3
            for bit in range(nbits):
                size = sub << bit

                @pl.when(((n_tiles >> bit) & 1) == 1)
                def _():
                    off = pl.multiple_of(za + sub * (n_tiles & ((1 << bit) - 1)), sub)
                    act(pltpu.make_async_copy(zbuf.at[pl.ds(0, size)],
                                              xb_ref.at[pl.ds(off, size)], zsem))
            return carry

        lax.fori_loop(0, N_EXP, per_expert, 0)
        for jb in range(N_EXP):
            blk = nu_ref[0] + jb

            @pl.when(blk < nb)
            def _():
                off = pl.multiple_of(blk * tm, tm)
                act(pltpu.make_async_copy(zbuf, xb_ref.at[pl.ds(off, tm)], zsem))

    @pl.when(i == 0)
    def _():
        zbuf[...] = jnp.zeros_like(zbuf)
        zero_fill(True)

    def issue(tk_i, carry):
        for kk in range(TOP_K):
            _row_copy(h_ref, tk_i, xb_ref, dest_ref[tk_i * TOP_K + kk], sem).start(
                priority=kk % 2)
        return carry

    lax.fori_loop(0, tk, issue, 0, unroll=8)

    def drain(tk_i, carry):
        for kk in range(TOP_K):
            _row_copy(h_ref, 0, xb_ref, 0, sem).wait()
        return carry

    lax.fori_loop(0, tk, drain, 0, unroll=8)

    @pl.when(i == pl.num_programs(0) - 1)
    def _():
        zero_fill(False)


def _dispatch(h2, dest_flat, pad_zero_start, pad_cnt, n_used, tm, nb):
    t = h2.shape[0]
    tk = min(TK_TOK, t)
    grid_spec = pltpu.PrefetchScalarGridSpec(
        num_scalar_prefetch=3,
        grid=(t // tk,),
        in_specs=[pl.BlockSpec((tk * TOP_K,), lambda i, *_: (i,), memory_space=pltpu.SMEM),
                  pl.BlockSpec((tk, D_MODEL), lambda i, *_: (i, 0))],
        out_specs=pl.BlockSpec(memory_space=pl.ANY),
        scratch_shapes=[pltpu.VMEM((tm, D_MODEL), F32),
                        pltpu.SemaphoreType.DMA(()), pltpu.SemaphoreType.DMA(())],
    )
    return pl.pallas_call(
        functools.partial(_dispatch_kernel, tm=tm, nb=nb),
        grid_spec=grid_spec,
        out_shape=jax.ShapeDtypeStruct((nb * tm, D_MODEL), F32),
        compiler_params=pltpu.CompilerParams(
            dimension_semantics=("arbitrary",), vmem_limit_bytes=VMEM_LIMIT,
            has_side_effects=True),
        name="dispatch",
    )(pad_zero_start, pad_cnt, n_used, dest_flat, h2)


def _expert_kernel(be_ref, slot_ref, nxt_ref, nu_ref, xb_ref, w1_ref, b1_ref, w2_ref, b2_ref,
                   yb_ref, w1f, w2f, w1b, w2b, sem1, sem2):
    i = pl.program_id(0)
    used = i < nu_ref[0]
    e = be_ref[i]
    slot = slot_ref[i]
    new_expert = (i == 0) | (e != be_ref[jnp.maximum(i - 1, 0)])

    def fetch(expert, slot_):
        return (pltpu.make_async_copy(w1_ref.at[expert], w1f.at[slot_], sem1.at[slot_]),
                pltpu.make_async_copy(w2_ref.at[expert], w2f.at[slot_], sem2.at[slot_]))

    @pl.when(used & (i == 0))
    def _():
        for cp in fetch(e, slot):
            cp.start()

    @pl.when(used & new_expert)
    def _():
        for cp in fetch(e, slot):
            cp.wait()
        w1b[...] = w1f[slot].astype(BF16)
        w2b[...] = w2f[slot].astype(BF16)
        nxt = nxt_ref[i]

        @pl.when(nxt >= 0)
        def _():
            for cp in fetch(nxt, 1 - slot):
                cp.start()

    @pl.when(used)
    def _():
        x = xb_ref[...].astype(BF16)
        hh = _dot(x, w1b[...]) + b1_ref[0]
        x_glu = jnp.minimum(hh[:, :D_FF], SWIGLU_LIMIT)
        x_lin = jnp.clip(hh[:, D_FF:], -SWIGLU_LIMIT, SWIGLU_LIMIT)
        act = x_glu * jax.nn.sigmoid(SWIGLU_ALPHA * x_glu) * (x_lin + 1.0)
        yb_ref[...] = _dot(act.astype(BF16), w2b[...]) + b2_ref[0]

    @pl.when(i >= nu_ref[0])
    def _():
        yb_ref[...] = jnp.zeros_like(yb_ref)


def _experts(xb, block_expert, block_slot, block_next, n_used, w1, b1, w2, b2, tm, nb):
    def x_map(i, be, sl, nx, nu):
        return (jnp.minimum(i, nu[0] - 1), 0)

    def b_map(i, be, sl, nx, nu):
        return (be[i], 0, 0)

    grid_spec = pltpu.PrefetchScalarGridSpec(
        num_scalar_prefetch=4,
        grid=(nb,),
        in_specs=[pl.BlockSpec((tm, D_MODEL), x_map),
                  pl.BlockSpec(memory_space=pl.ANY),
                  pl.BlockSpec((1, 1, 2 * D_FF), b_map),
                  pl.BlockSpec(memory_space=pl.ANY),
                  pl.BlockSpec((1, 1, D_MODEL), b_map)],
        out_specs=pl.BlockSpec((tm, D_MODEL), lambda i, be, sl, nx, nu: (i, 0)),
        scratch_shapes=[pltpu.VMEM((2, D_MODEL, 2 * D_FF), F32),
                        pltpu.VMEM((2, D_FF, D_MODEL), F32),
                        pltpu.VMEM((D_MODEL, 2 * D_FF), BF16),
                        pltpu.VMEM((D_FF, D_MODEL), BF16),
                        pltpu.SemaphoreType.DMA((2,)), pltpu.SemaphoreType.DMA((2,))],
    )
    return pl.pallas_call(
        _expert_kernel,
        grid_spec=grid_spec,
        out_shape=jax.ShapeDtypeStruct((nb * tm, D_MODEL), F32),
        compiler_params=pltpu.CompilerParams(
            dimension_semantics=("arbitrary",), vmem_limit_bytes=VMEM_LIMIT),
        name="experts",
    )(block_expert, block_slot, block_next, n_used, xb, w1, b1, w2, b2)


def _combine_kernel(dest_ref, dnext_ref, h_ref, gate_ref, g_ref, b_ref, yb_ref, o_ref, ybuf, sem):
    i = pl.program_id(0)
    tk = h_ref.shape[0]
    slot = i % 2

    def gather(idx_ref, slot_):
        def issue(tk_i, carry):
            for kk in range(TOP_K):
                _row_copy(yb_ref, idx_ref[tk_i * TOP_K + kk], ybuf.at[slot_, kk], tk_i,
                          sem.at[slot_]).start(priority=kk % 2)
            return carry

        lax.fori_loop(0, tk, issue, 0, unroll=8)

    @pl.when(i == 0)
    def _():
        gather(dest_ref, 0)

    def drain(tk_i, carry):
        for kk in range(TOP_K):
            _row_copy(yb_ref, 0, ybuf.at[slot, kk], 0, sem.at[slot]).wait()
        return carry

    lax.fori_loop(0, tk, drain, 0, unroll=8)

    @pl.when(i + 1 < pl.num_programs(0))
    def _():
        gather(dnext_ref, 1 - slot)

    ff = jnp.zeros((tk, D_MODEL), F32)
    for kk in range(TOP_K):
        ff = ff + gate_ref[:, kk:kk + 1] * ybuf[slot, kk]
    o_ref[...] = _layer_norm(ALPHA * h_ref[...] + ff, g_ref[...], b_ref[...])


def _combine(dest_flat, h2, gates, g3, b3, yb):
    t = h2.shape[0]
    tk = min(TK_TOK, t)
    last = t // tk - 1
    return pl.pallas_call(
        _combine_kernel,
        grid=(t // tk,),
        in_specs=[pl.BlockSpec((tk * TOP_K,), lambda i: (i,), memory_space=pltpu.SMEM),
                  pl.BlockSpec((tk * TOP_K,), lambda i: (jnp.minimum(i + 1, last),),
                               memory_space=pltpu.SMEM),
                  pl.BlockSpec((tk, D_MODEL), lambda i: (i, 0)),
                  pl.BlockSpec((tk, TOP_K), lambda i: (i, 0)),
                  pl.BlockSpec((1, D_MODEL), lambda i: (0, 0)),
                  pl.BlockSpec((1, D_MODEL), lambda i: (0, 0)),
                  pl.BlockSpec(memory_space=pl.ANY)],
        out_specs=pl.BlockSpec((tk, D_MODEL), lambda i: (i, 0)),
        out_shape=jax.ShapeDtypeStruct((t, D_MODEL), F32),
        scratch_shapes=[pltpu.VMEM((2, TOP_K, tk, D_MODEL), F32), pltpu.SemaphoreType.DMA((2,))],
        compiler_params=pltpu.CompilerParams(
            dimension_semantics=("arbitrary",), vmem_limit_bytes=VMEM_LIMIT),
        name="combine",
    )(dest_flat, dest_flat, h2, gates, g3, b3, yb)


def _row(v):
    return v.astype(F32).reshape(1, -1)


def kernel(x, mem, ln_in_g, ln_in_b, w_in, hgrn_lb_logits, hgrn_norm_g, s5_lam_re, s5_lam_im,
           s5_log_dt, s5_b_re, s5_b_im, s5_c_re, s5_c_im, s5_d, s5_w_glu, s5_b_glu, w_out,
           ln1_g, ln1_b, xa_w_q, xa_w_k, xa_w_v, xa_w_o, ln2_g, ln2_b, router_w, router_b,
           exp_w1, exp_b1, exp_w2, exp_b2, ln3_g, ln3_b):
    batch, seq, _ = x.shape
    t = batch * seq
    lb_all = jnp.cumsum(jax.nn.softmax(hgrn_lb_logits.astype(F32), axis=0), axis=0)
    h = x.reshape(t, D_MODEL)
    assert DEPTH == 1 and w_in.shape[0] == 1
    for l in range(DEPTH):
        h0, qfig, u = _ln_inproj(h, _row(ln_in_g), _row(ln_in_b), w_in[l].astype(BF16))
        o_a = _hgrn2(qfig, lb_all[l].reshape(1, HG_W), _row(hgrn_norm_g[l]), batch, seq)
        s5t = _s5_tables(s5_lam_re[l], s5_lam_im[l], s5_log_dt[l], s5_b_re[l], s5_b_im[l],
                         s5_c_re[l], s5_c_im[l], s5_d[l], seq // S5_L)
        y5 = _s5(u, s5t, batch, seq)
        wkv = jnp.concatenate([xa_w_k[l], xa_w_v[l]], axis=1).astype(BF16)
        kmem, vmem = _kv_proj(mem.reshape(batch * N_MEM, D_MODEL), wkv)
        kmem = kmem.reshape(batch, N_MEM, D_MODEL)
        vmem = vmem.reshape(batch, N_MEM, D_MODEL)
        wr = jnp.pad(router_w[l].astype(F32), ((0, 0), (0, LANES - N_EXP)))
        wr3 = jnp.stack(_split3(wr)[:2], axis=0)
        br = jnp.pad(router_b[l].astype(F32), (0, LANES - N_EXP)).reshape(1, LANES)
        h2, top_idx, gates, rank, cnt = _mid(
            o_a, y5, h0, kmem, vmem, s5_w_glu[l].astype(BF16), _row(s5_b_glu[l]),
            w_out[l].astype(BF16), _row(ln1_g[l]), _row(ln1_b[l]), xa_w_q[l].astype(BF16),
            xa_w_o[l].astype(BF16), _row(ln2_g[l]), _row(ln2_b[l]), wr3, br, batch, seq)

        tm = min(TM_MOE, t)
        nb = (t * TOP_K) // tm + N_EXP
        counts = cnt[0, :N_EXP]
        padded = (counts + tm - 1) // tm * tm
        pad_end = jnp.cumsum(padded)
        pad_start = pad_end - padded
        dest = (pad_start[top_idx.reshape(t * TOP_K)] + rank.reshape(t * TOP_K)).astype(jnp.int32)
        n_used = (pad_end[-1:] // tm).astype(jnp.int32)
        block_start = jnp.arange(nb, dtype=jnp.int32) * tm
        block_expert = jnp.sum(
            pad_end[None, :] <= jnp.minimum(block_start, pad_end[-1] - tm)[:, None],
            axis=1).astype(jnp.int32)
        nonempty = counts > 0
        order = jnp.cumsum(nonempty.astype(jnp.int32)) - 1
        eid = jnp.arange(N_EXP, dtype=jnp.int32)
        later = jnp.where(nonempty[None, :] & (eid[None, :] > eid[:, None]), eid[None, :], N_EXP)
        succ = jnp.min(later, axis=1)
        succ = jnp.where(succ < N_EXP, succ, -1).astype(jnp.int32)
        block_slot = (order[block_expert] % 2).astype(jnp.int32)
        block_next = succ[block_expert]
        xb = _dispatch(h2, dest, (pad_start + counts).astype(jnp.int32),
                       (padded - counts).astype(jnp.int32), n_used, tm, nb)
        yb = _experts(xb, block_expert, block_slot, block_next, n_used, exp_w1[l],
                      exp_b1[l].reshape(N_EXP, 1, 2 * D_FF), exp_w2[l],
                      exp_b2[l].reshape(N_EXP, 1, D_MODEL), tm, nb)
        h = _combine(dest, h2, gates, _row(ln3_g[l]), _row(ln3_b[l]), yb)
    return h.reshape(batch, seq, D_MODEL)
```

```python
import functools
import math

import jax
import jax.numpy as jnp
from jax import lax
from jax.experimental import pallas as pl
from jax.experimental.pallas import tpu as pltpu

F32 = jnp.float32
BF16 = jnp.bfloat16

D_MODEL = 1024
HG_W = 512
HG_D = 128
HG_H = HG_W // HG_D
HG_CHUNK = 64
HG_SUB = 16
S5_W = 512
S5_N = 16
S5_G = S5_W // S5_N
S5_P = 64
S5_L = 64
S5_TB = 8
IN_COLS = 4 * HG_W + S5_W
N_MEM = 256
XA_H = 4
XA_D = D_MODEL // XA_H
XA_SCALE = XA_D ** -0.5
N_EXP = 32
TOP_K = 4
D_FF = D_MODEL
SWIGLU_LIMIT = 7.0
SWIGLU_ALPHA = 1.702
DEPTH = 1
ALPHA = (2 * DEPTH) ** 0.25
LN_EPS = 1e-5
RMS_EPS = 1e-6
LANES = 128

TM_IN = 512
TQ_MID = 512
TM_MOE = 256
TK_TOK = 256
VMEM_LIMIT = 56 * 1024 * 1024


def _layer_norm(x, g, b):
    mu = jnp.mean(x, axis=-1, keepdims=True)
    xc = x - mu
    var = jnp.mean(xc * xc, axis=-1, keepdims=True)
    return xc * lax.rsqrt(var + LN_EPS) * g + b


def _dot(a, b):
    return jnp.dot(a, b, preferred_element_type=F32)


def _dot_nt(a, b):
    return lax.dot_general(a, b, (((1,), (1,)), ((), ())), preferred_element_type=F32)


def _dot_tn(a, b):
    return lax.dot_general(a, b, (((0,), (0,)), ((), ())), preferred_element_type=F32)


def _split3(x):
    hi = x.astype(BF16)
    r1 = x - hi.astype(F32)
    mid = r1.astype(BF16)
    lo = (r1 - mid.astype(F32)).astype(BF16)
    return hi, mid, lo


def _ln_inproj_kernel(x_ref, g_ref, b_ref, w_ref, h_ref, qfig_ref, u_ref):
    h = _layer_norm(x_ref[...], g_ref[...], b_ref[...])
    h_ref[...] = h
    p = _dot(h.astype(BF16), w_ref[...])
    qfig_ref[...] = p[:, :4 * HG_W]
    u_ref[...] = p[:, 4 * HG_W:].astype(u_ref.dtype)


def _ln_inproj(x2, g, b, w_bf):
    t = x2.shape[0]
    tm = min(TM_IN, t)
    return pl.pallas_call(
        _ln_inproj_kernel,
        grid=(t // tm,),
        in_specs=[
            pl.BlockSpec((tm, D_MODEL), lambda i: (i, 0)),
            pl.BlockSpec((1, D_MODEL), lambda i: (0, 0)),
            pl.BlockSpec((1, D_MODEL), lambda i: (0, 0)),
            pl.BlockSpec((D_MODEL, IN_COLS), lambda i: (0, 0)),
        ],
        out_specs=[
            pl.BlockSpec((tm, D_MODEL), lambda i: (i, 0)),
            pl.BlockSpec((tm, 4 * HG_W), lambda i: (i, 0)),
            pl.BlockSpec((tm, S5_W), lambda i: (i, 0)),
        ],
        out_shape=[
            jax.ShapeDtypeStruct((t, D_MODEL), F32),
            jax.ShapeDtypeStruct((t, 4 * HG_W), F32),
            jax.ShapeDtypeStruct((t, S5_W), BF16),
        ],
        compiler_params=pltpu.CompilerParams(
            dimension_semantics=("arbitrary",), vmem_limit_bytes=VMEM_LIMIT),
        name="ln_inproj",
    )(x2, g, b, w_bf)


def _hgrn_kernel(q_ref, f_ref, i_ref, g_ref, lb_ref, ng_ref, o_ref, st_ref, k_s, cum_s):
    nseq = q_ref.shape[0]
    rows = nseq * HG_CHUNK

    @pl.when(pl.program_id(0) == 0)
    def _():
        st_ref[...] = jnp.zeros_like(st_ref)

    lb = lb_ref[...]
    f = lb + (1.0 - lb) * jax.nn.sigmoid(f_ref[...].reshape(rows, HG_W))
    k_s[...] = 1.0 - f
    r = lax.broadcasted_iota(jnp.int32, (rows, rows), 0)
    c = lax.broadcasted_iota(jnp.int32, (rows, rows), 1)
    tri = ((c <= r) & ((r // HG_CHUNK) == (c // HG_CHUNK))).astype(BF16)
    hi, mid, lo = _split3(jnp.log2(f))
    cum_s[...] = _dot(tri, hi) + _dot(tri, mid) + _dot(tri, lo)

    row_in_sub = lax.broadcasted_iota(jnp.int32, (HG_SUB, 1), 0)
    ng = ng_ref[...]
    for b in range(nseq):
        r0 = b * HG_CHUNK
        for h in range(HG_H):
            cols = slice(h * HG_D, (h + 1) * HG_D)
            qc = q_ref[b, :, cols]
            kc = k_s[r0:r0 + HG_CHUNK, cols]
            vc = i_ref[b, :, cols]
            cc = cum_s[r0:r0 + HG_CHUNK, cols]
            last = cum_s[r0 + HG_CHUNK - 1:r0 + HG_CHUNK, cols]
            st = st_ref[b, h]
            inter = _dot_nt((qc * jnp.exp2(cc)).astype(BF16), st.astype(BF16))
            vb = vc.astype(BF16)
            outs = []
            for blk in range(HG_CHUNK // HG_SUB):
                rs = slice(HG_SUB * blk, HG_SUB * (blk + 1))
                q_i = qc[rs]
                c_i = cc[rs]
                acc = inter[rs]
                if blk > 0:
                    n = HG_SUB * blk
                    c_ref = cum_s[r0 + n - 1:r0 + n, cols]
                    khat = (kc[:n] * jnp.exp2(c_ref - cc[:n])).astype(BF16)
                    qhat = (q_i * jnp.exp2(c_i - c_ref)).astype(BF16)
                    acc = acc + _dot(_dot_nt(qhat, khat).astype(BF16), vb[:n])
                for jj in range(HG_SUB):
                    srow = HG_SUB * blk + jj
                    c_s = cum_s[r0 + srow:r0 + srow + 1, cols]
                    k_row = k_s[r0 + srow:r0 + srow + 1, cols]
                    v_row = i_ref[b, srow:srow + 1, cols]
                    p = q_i * k_row * jnp.exp2(c_i - c_s)
                    w = jnp.sum(p, axis=-1, keepdims=True)
                    w = jnp.where(row_in_sub >= jj, w, 0.0)
                    acc = acc + w * v_row
                outs.append(acc)
            o = jnp.concatenate(outs, axis=0)
            kd = (kc * jnp.exp2(last - cc)).astype(BF16)
            st_ref[b, h] = st * jnp.exp2(last) + _dot_tn(vb, kd)
            o = o * lax.rsqrt(jnp.mean(o * o, axis=-1, keepdims=True) + RMS_EPS) * ng
            gg = g_ref[b, :, cols]
            o_ref[b, :, cols] = (o * (gg * jax.nn.sigmoid(gg))).astype(o_ref.dtype)


def _hgrn2(qfig, lb, ng, batch, seq):
    nj = seq // HG_CHUNK
    qfig3 = qfig.reshape(batch, seq, 4 * HG_W)

    def col_spec(cb):
        return pl.BlockSpec((batch, HG_CHUNK, HG_W), lambda j: (0, j, cb))

    out = pl.pallas_call(
        _hgrn_kernel,
        grid=(nj,),
        in_specs=[col_spec(0), col_spec(1), col_spec(2), col_spec(3),
                  pl.BlockSpec((1, HG_W), lambda j: (0, 0)),
                  pl.BlockSpec((1, HG_D), lambda j: (0, 0))],
        out_specs=pl.BlockSpec((batch, HG_CHUNK, HG_W), lambda j: (0, j, 0)),
        out_shape=jax.ShapeDtypeStruct((batch, seq, HG_W), BF16),
        scratch_shapes=[pltpu.VMEM((batch, HG_H, HG_D, HG_D), F32),
                        pltpu.VMEM((batch * HG_CHUNK, HG_W), F32),
                        pltpu.VMEM((batch * HG_CHUNK, HG_W), F32)],
        compiler_params=pltpu.CompilerParams(
            dimension_semantics=("arbitrary",), vmem_limit_bytes=VMEM_LIMIT),
        name="hgrn2",
    )(qfig3, qfig3, qfig3, qfig3, lb, ng)
    return out.reshape(batch * seq, HG_W)


def _s5_kernel(u_ref, k2_ref, hm_ref, gm_ref, ab_ref, d_ref, y_ref, *, nc):
    ub = u_ref[0]
    rows = ub.shape[0]
    nblk = S5_L // S5_TB
    blocks = []
    for dlt in reversed(range(nblk)):
        for s in range(S5_TB):
            off = S5_N * (S5_TB * dlt - s + S5_TB - 1)
            base = off // LANES * LANES
            win = k2_ref[0, :, base:base + 2 * LANES]
            if off != base:
                win = pltpu.roll(win, 2 * LANES - (off - base), axis=1)
            blocks.append(win[:, :LANES].astype(BF16))
    tt_rev = jnp.concatenate(blocks, axis=0)
    ys = []
    for bc in range(nblk):
        ys.append(_dot(ub[:, :(bc + 1) * LANES], tt_rev[(nblk - 1 - bc) * LANES:, :]))
    y = jnp.concatenate(ys, axis=1)
    x = _dot(ub, hm_ref[0])
    chunk_id = lax.broadcasted_iota(jnp.int32, (rows, 2 * S5_P), 0) % nc
    lev = 0
    while (1 << lev) < nc:
        sh = 1 << lev
        a = ab_ref[0, 2 * lev:2 * lev + 1, :]
        bv = ab_ref[0, 2 * lev + 1:2 * lev + 2, :]
        xs = jnp.where(chunk_id >= sh, pltpu.roll(x, sh, axis=0), 0.0)
        x = x + xs * a + pltpu.roll(xs, S5_P, axis=1) * bv
        lev += 1
    prev = jnp.where(chunk_id >= 1, pltpu.roll(x, 1, axis=0), 0.0)
    y = y + _dot(prev.astype(BF16), gm_ref[0])
    y = y + d_ref[0] * ub.astype(F32)
    y_ref[0] = jax.nn.gelu(y).astype(y_ref.dtype)


def _s5_tables(lam_re, lam_im, log_dt, b_re, b_im, c_re, c_im, d_skip, nc):
    hp = lax.Precision.HIGHEST
    lam_r, lam_i = lam_re.astype(F32), lam_im.astype(F32)
    dt = jnp.exp(log_dt.astype(F32))
    z_r, z_i = lam_r * dt, lam_i * dt

    def lam_pow(zr, zi, tau):
        mag = jnp.exp(zr * tau)
        return mag * jnp.cos(zi * tau), mag * jnp.sin(zi * tau)

    taus = jnp.arange(S5_L + 1, dtype=F32)[None, :, None]
    p_r, p_i = lam_pow(z_r[:, None, :], z_i[:, None, :], taus)
    x_r, x_i = p_r[:, 1] - 1.0, p_i[:, 1]
    den = lam_r * lam_r + lam_i * lam_i
    f_r = ((x_r * lam_r + x_i * lam_i) / den)[..., None]
    f_i = ((x_i * lam_r - x_r * lam_i) / den)[..., None]
    b_r, b_i = b_re.astype(F32), b_im.astype(F32)
    bb_r = f_r * b_r - f_i * b_i
    bb_i = f_r * b_i + f_i * b_r
    lane = jnp.arange(S5_L * S5_N)
    sel_t = (lane[None, :] // S5_N == jnp.arange(S5_L)[:, None]).astype(F32)
    sel_n = (lane[None, :] % S5_N == jnp.arange(S5_N)[:, None]).astype(F32)
    ce_r = jnp.einsum('gnp,nl->gpl', c_re.astype(F32), sel_n, precision=hp)
    ce_i = jnp.einsum('gnp,nl->gpl', c_im.astype(F32), sel_n, precision=hp)
    t_r = jnp.einsum('gtp,tl->gpl', p_r[:, :S5_L], sel_t, precision=hp)
    t_i = jnp.einsum('gtp,tl->gpl', p_i[:, :S5_L], sel_t, precision=hp)
    k2 = (jnp.einsum('gpm,gpl->gml', bb_r, ce_r * t_r - ce_i * t_i, precision=hp)
          - jnp.einsum('gpm,gpl->gml', bb_i, ce_r * t_i + ce_i * t_r, precision=hp))
    lead = (S5_TB - 1) * S5_N
    k2 = jnp.pad(k2, ((0, 0), (0, 0), (lead, LANES - lead)))
    q_r = p_r[:, S5_L - 1::-1][:, :, None, :]
    q_i = p_i[:, S5_L - 1::-1][:, :, None, :]
    bt_r = bb_r.transpose(0, 2, 1)[:, None]
    bt_i = bb_i.transpose(0, 2, 1)[:, None]
    hm = jnp.concatenate([q_r * bt_r - q_i * bt_i, q_r * bt_i + q_i * bt_r], axis=-1)
    hm = hm.reshape(S5_G, S5_L * S5_N, 2 * S5_P)
    e_r = jnp.einsum('gtp,tl->gpl', p_r[:, 1:], sel_t, precision=hp)
    e_i = jnp.einsum('gtp,tl->gpl', p_i[:, 1:], sel_t, precision=hp)
    gm =jnp.concatenate([ce_r * e_r - ce_i * e_i, -(ce_r * e_i + ce_i * e_r)], axis=1)
    rows = []
    lev = 0
    while (1 << lev) < nc:
        a, b = lam_pow(z_r, z_i, float(S5_L * (1 << lev)))
        rows.append(jnp.concatenate([a, a], axis=-1))
        rows.append(jnp.concatenate([-b, b], axis=-1))
        lev += 1
    if not rows:
        rows = [jnp.zeros((S5_G, 2 * S5_P), F32)] * 2
    ab = jnp.stack(rows, axis=1)
    dvec = jnp.einsum('gn,nl->gl', d_skip.astype(F32).reshape(S5_G, S5_N), sel_n, precision=hp)
    return k2, hm.astype(BF16), gm.astype(BF16), ab, dvec.reshape(S5_G, 1, S5_L * S5_N)


def _s5(u, tables, batch, seq):
    k2, hm, gm, ab, dvec = tables
    nc = seq // S5_L
    rows = batch * nc
    width = S5_L * S5_N
    ug = u.reshape(batch, nc, S5_L, S5_G, S5_N).transpose(3, 0, 1, 2, 4).reshape(S5_G, rows, width)
    nab = ab.shape[1]
    yg = pl.pallas_call(
        functools.partial(_s5_kernel, nc=nc),
        grid=(S5_G,),
        in_specs=[
            pl.BlockSpec((1, rows, width), lambda g: (g, 0, 0)),
            pl.BlockSpec((1, S5_N, k2.shape[2]), lambda g: (g, 0, 0)),
            pl.BlockSpec((1, width, 2 * S5_P), lambda g: (g, 0, 0)),
            pl.BlockSpec((1, 2 * S5_P, width), lambda g: (g, 0, 0)),
            pl.BlockSpec((1, nab, 2 * S5_P), lambda g: (g, 0, 0)),
            pl.BlockSpec((1, 1, width), lambda g: (g, 0, 0)),
        ],
        out_specs=pl.BlockSpec((1, rows, width), lambda g: (g, 0, 0)),
        out_shape=jax.ShapeDtypeStruct((S5_G, rows, width), BF16),
        compiler_params=pltpu.CompilerParams(
            dimension_semantics=("arbitrary",), vmem_limit_bytes=VMEM_LIMIT),
        name="s5",
    )(ug, k2, hm, gm, ab, dvec)
    return yg.reshape(S5_G, batch, nc, S5_L, S5_N).transpose(1, 2, 3, 0, 4).reshape(
        batch * seq, S5_W)


def _kv_kernel(m_ref, w_ref, k_ref, v_ref):
    kv = _dot(m_ref[...].astype(BF16), w_ref[...])
    k_ref[...] = kv[:, :D_MODEL].astype(BF16)
    v_ref[...] = kv[:, D_MODEL:].astype(BF16)


def _kv_proj(mem2, wkv_bf):
    rows = mem2.shape[0]
    tm = min(256, rows)
    return pl.pallas_call(
        _kv_kernel,
        grid=(rows // tm,),
        in_specs=[pl.BlockSpec((tm, D_MODEL), lambda i: (i, 0)),
                  pl.BlockSpec((D_MODEL, 2 * D_MODEL), lambda i: (0, 0))],
        out_specs=[pl.BlockSpec((tm, D_MODEL), lambda i: (i, 0)),
                   pl.BlockSpec((tm, D_MODEL), lambda i: (i, 0))],
        out_shape=[jax.ShapeDtypeStruct((rows, D_MODEL), BF16),
                   jax.ShapeDtypeStruct((rows, D_MODEL), BF16)],
        compiler_params=pltpu.CompilerParams(
            dimension_semantics=("arbitrary",), vmem_limit_bytes=VMEM_LIMIT),
        name="kv_proj",
    )(mem2, wkv_bf)


def _mid_kernel(oa_ref, y5_ref, h0_ref, k_ref, v_ref, wglu_ref, bglu_ref, wout_ref,
                g1_ref, b1_ref, wq_ref, wo_ref, g2_ref, b2_ref, wr_ref, br_ref,
                h2_ref, idx_ref, gate_ref, rank_ref, cnt_ref, run_ref):
    tq = oa_ref.shape[0]

    @pl.when((pl.program_id(0) == 0) & (pl.program_id(1) == 0))
    def _():
        run_ref[...] = jnp.zeros_like(run_ref)

    y5b = y5_ref[...]
    ob = y5b.astype(F32) * jax.nn.sigmoid(_dot(y5b, wglu_ref[...]) + bglu_ref[...])
    mix = _dot(oa_ref[...], wout_ref[:HG_W, :]) + _dot(ob.astype(BF16), wout_ref[HG_W:, :])
    h1 = _layer_norm(ALPHA * h0_ref[...] + mix, g1_ref[...], b1_ref[...])

    q = _dot(h1.astype(BF16), wq_ref[...])
    heads = []
    for hd in range(XA_H):
        cs = slice(hd * XA_D, (hd + 1) * XA_D)
        s = _dot_nt(q[:, cs].astype(BF16), k_ref[0, :, cs]) * XA_SCALE
        e = jnp.exp(s - jnp.max(s, axis=-1, keepdims=True))
        p = e / jnp.sum(e, axis=-1, keepdims=True)
        heads.append(_dot(p.astype(BF16), v_ref[0, :, cs]))
    xa = _dot(jnp.concatenate(heads, axis=1).astype(BF16), wo_ref[...])
    h2 = _layer_norm(ALPHA * h1 + xa, g2_ref[...], b2_ref[...])
    h2_ref[...] = h2

    h_hi, h_mid, _ = _split3(h2)
    w_hi, w_mid = wr_ref[0], wr_ref[1]
    logits = (_dot(h_hi, w_hi) + _dot(h_mid, w_hi) + _dot(h_hi, w_mid)) + br_ref[...]
    lane = lax.broadcasted_iota(jnp.int32, (tq, LANES), 1)
    lane_f = lane.astype(F32)
    work = jnp.where(lane < N_EXP, logits, -jnp.inf)
    vals, onehots = [], []
    for kk in range(TOP_K):
        mx = jnp.max(work, axis=-1, keepdims=True)
        sel = jnp.min(jnp.where(work == mx, lane_f, float(LANES)), axis=-1, keepdims=True)
        oh = lane_f == sel
        idx_ref[:, kk:kk + 1] = sel.astype(jnp.int32)
        vals.append(mx)
        onehots.append(oh)
        work = jnp.where(oh, -jnp.inf, work)
    es = [jnp.exp(v - vals[0]) for v in vals]
    den = es[0] + es[1] + es[2] + es[3]
    for kk in range(TOP_K):
        gate_ref[:, kk:kk + 1] = es[kk] / den

    member = jnp.zeros((tq, LANES), F32)
    for oh in onehots:
        member = member + oh.astype(F32)
    r = lax.broadcasted_iota(jnp.int32, (tq, tq), 0)
    c = lax.broadcasted_iota(jnp.int32, (tq, tq), 1)
    before = _dot((c < r).astype(BF16), member.astype(BF16)) + run_ref[...]
    for kk in range(TOP_K):
        rk = jnp.sum(jnp.where(onehots[kk], before, 0.0), axis=-1, keepdims=True)
        rank_ref[:, kk:kk + 1] = rk.astype(jnp.int32)
    run = run_ref[...] + jnp.sum(member, axis=0, keepdims=True)
    run_ref[...] = run
    cnt_ref[...] = run.astype(jnp.int32)


def _mid(oa, y5, h0, kmem, vmem, wglu, bglu, wout, g1, b1, wq, wo, g2, b2, wr3, br, batch, seq):
    t = oa.shape[0]
    tq = min(TQ_MID, seq)
    nj = seq // tq

    def row_spec(w):
        return pl.BlockSpec((tq, w), lambda b, j: (b * nj + j, 0))

    def full_spec(shape):
        nd = len(shape)
        return pl.BlockSpec(shape, lambda b, j: (0,) * nd)

    mem_spec = pl.BlockSpec((1, N_MEM, D_MODEL), lambda b, j: (b, 0, 0))
    return pl.pallas_call(
        _mid_kernel,
        grid=(batch, nj),
        in_specs=[row_spec(HG_W), row_spec(S5_W), row_spec(D_MODEL), mem_spec, mem_spec,
                  full_spec((S5_W, S5_W)), full_spec((1, S5_W)),
                  full_spec((D_MODEL, D_MODEL)), full_spec((1, D_MODEL)), full_spec((1, D_MODEL)),
                  full_spec((D_MODEL, D_MODEL)), full_spec((D_MODEL, D_MODEL)),
                  full_spec((1, D_MODEL)), full_spec((1, D_MODEL)),
                  full_spec((2, D_MODEL, LANES)), full_spec((1, LANES))],
        out_specs=[row_spec(D_MODEL), row_spec(TOP_K), row_spec(TOP_K), row_spec(TOP_K),
                   full_spec((1, LANES))],
        out_shape=[jax.ShapeDtypeStruct((t, D_MODEL), F32),
                   jax.ShapeDtypeStruct((t, TOP_K), jnp.int32),
                   jax.ShapeDtypeStruct((t, TOP_K), F32),
                   jax.ShapeDtypeStruct((t, TOP_K), jnp.int32),
                   jax.ShapeDtypeStruct((1, LANES), jnp.int32)],
        scratch_shapes=[pltpu.VMEM((1, LANES), F32)],
        compiler_params=pltpu.CompilerParams(
            dimension_semantics=("arbitrary", "arbitrary"), vmem_limit_bytes=VMEM_LIMIT),
        name="mid",
    )(oa, y5, h0, kmem, vmem, wglu, bglu, wout, g1, b1, wq, wo, g2, b2, wr3, br)


def _dest_kernel(ps_ref, idx_ref, rank_ref, o_ref):
    idx = idx_ref[...]
    acc = rank_ref[...]
    for e in range(N_EXP):
        acc = acc + jnp.where(idx == e, ps_ref[e], 0)
    o_ref[...] = acc


def _dest_rows(pad_start, top_idx, rank):
    n = top_idx.size
    shape = (n // LANES, LANES)
    grid_spec = pltpu.PrefetchScalarGridSpec(
        num_scalar_prefetch=1,
        grid=(1,),
        in_specs=[pl.BlockSpec(shape, lambda i, ps: (0, 0)),
                  pl.BlockSpec(shape, lambda i, ps: (0, 0))],
        out_specs=pl.BlockSpec(shape, lambda i, ps: (0, 0)),
    )
    dest = pl.pallas_call(
        _dest_kernel,
        grid_spec=grid_spec,
        out_shape=jax.ShapeDtypeStruct(shape, jnp.int32),
        name="dest_rows",
    )(pad_start, top_idx.reshape(shape), rank.reshape(shape))
    return dest.reshape(n)


def _row_copy(src, src_row, dst, dst_row, sem):
    return pltpu.make_async_copy(src.at[pl.ds(src_row, 1)], dst.at[pl.ds(dst_row, 1)], sem)


def _dispatch_kernel(pz_ref, pc_ref, nu_ref, dest_ref, h_ref, xb_ref, zbuf, sem, zsem, *, tm, nb):
    i = pl.program_id(0)
    tk = h_ref.shape[0]
    sub = 8
    nbits = (tm // sub).bit_length() - 1

    def zero_fill(start_not_wait):
        def act(cp):
            if start_not_wait:
                cp.start()
            else:
                cp.wait()

        def per_expert(e, carry):
            cnt = pc_ref[e]
            z0 = pz_ref[e]
            head = (-z0) & (sub - 1)
            for rr in range(sub - 1):
                @pl.when(rr < head)
                def _():
                    act(_row_copy(zbuf, 0, xb_ref, z0 + rr, zsem))
            za = z0 + head
            n_tiles = (cnt - head) >> 3
            for bit in range(nbits):
                size = sub << bit

                @pl.when(((n_tiles >> bit) & 1) == 1)
                def _():
                    off = pl.multiple_of(za + sub * (n_tiles & ((1 << bit) - 1)), sub)
                    act(pltpu.make_async_copy(zbuf.at[pl.ds(0, size)],
                                              xb_ref.at[pl.ds(off, size)], zsem))
            return carry

        lax.fori_loop(0, N_EXP, per_expert, 0)
        for jb in range(N_EXP):
            blk = nu_ref[0] + jb

            @pl.when(blk < nb)
            def _():
                off = pl.multiple_of(blk * tm, tm)
                act(pltpu.make_async_copy(zbuf, xb_ref.at[pl.ds(off, tm)], zsem))

    @pl.when(i == 0)
    def _():
        zbuf[...] = jnp.zeros_like(zbuf)
        zero_fill(True)

    def issue(tk_i, carry):
        for kk in range(TOP_K):
            _row_copy(h_ref, tk_i, xb_ref, dest_ref[tk_i * TOP_K + kk], sem).start(
                priority=kk % 2)
        return carry

    lax.fori_loop(0, tk, issue, 0, unroll=8)

    def drain(tk_i, carry):
        for kk in range(TOP_K):
            _row_copy(h_ref, 0, xb_ref, 0, sem).wait()
        return carry

    lax.fori_loop(0, tk, drain, 0, unroll=8)

    @pl.when(i == pl.num_programs(0) - 1)
    def _():
        zero_fill(False)


def _dispatch(h2, dest_flat, pad_zero_start, pad_cnt, n_used, tm, nb):
    t = h2.shape[0]
    tk = min(TK_TOK, t)
    grid_spec = pltpu.PrefetchScalarGridSpec(
        num_scalar_prefetch=3,
        grid=(t // tk,),
        in_specs=[pl.BlockSpec((tk * TOP_K,), lambda i, *_: (i,), memory_space=pltpu.SMEM),
                  pl.BlockSpec((tk, D_MODEL), lambda i, *_: (i, 0))],
        out_specs=pl.BlockSpec(memory_space=pl.ANY),
        scratch_shapes=[pltpu.VMEM((tm, D_MODEL), F32),
                        pltpu.SemaphoreType.DMA(()), pltpu.SemaphoreType.DMA(())],
    )
    return pl.pallas_call(
        functools.partial(_dispatch_kernel, tm=tm, nb=nb),
        grid_spec=grid_spec,
        out_shape=jax.ShapeDtypeStruct((nb * tm, D_MODEL), F32),
        compiler_params=pltpu.CompilerParams(
            dimension_semantics=("arbitrary",), vmem_limit_bytes=VMEM_LIMIT,
            has_side_effects=True),
        name="dispatch",
    )(pad_zero_start, pad_cnt, n_used, dest_flat, h2)


def _expert_kernel(be_ref, slot_ref, nxt_ref, nu_ref, xb_ref, w1_ref, b1_ref, w2_ref, b2_ref,
                   yb_ref, w1f, w2f, w1b, w2b, sem1, sem2):
    i = pl.program_id(0)
    used = i < nu_ref[0]
    e = be_ref[i]
    slot = slot_ref[i]
    new_expert = (i == 0) | (e != be_ref[jnp.maximum(i - 1, 0)])

    def fetch(expert, slot_):
        return (pltpu.make_async_copy(w1_ref.at[expert], w1f.at[slot_], sem1.at[slot_]),
                pltpu.make_async_copy(w2_ref.at[expert], w2f.at[slot_], sem2.at[slot_]))

    @pl.when(used & (i == 0))
    def _():
        for cp in fetch(e, slot):
            cp.start()

    @pl.when(used & new_expert)
    def _():
        for cp in fetch(e, slot):
            cp.wait()
        w1b[...] = w1f[slot].astype(BF16)
        w2b[...] = w2f[slot].astype(BF16)
        nxt = nxt_ref[i]

        @pl.when(nxt >= 0)
        def _():
            for cp in fetch(nxt, 1 - slot):
                cp.start()

    @pl.when(used)
    def _():
        x = xb_ref[...].astype(BF16)
        hh = _dot(x, w1b[...]) + b1_ref[0]
        x_glu = jnp.minimum(hh[:, :D_FF], SWIGLU_LIMIT)
        x_lin = jnp.clip(hh[:, D_FF:], -SWIGLU_LIMIT, SWIGLU_LIMIT)
        act = x_glu * jax.nn.sigmoid(SWIGLU_ALPHA * x_glu) * (x_lin + 1.0)
        yb_ref[...] = _dot(act.astype(BF16), w2b[...]) + b2_ref[0]

    @pl.when(i >= nu_ref[0])
    def _():
        yb_ref[...] = jnp.zeros_like(yb_ref)


def _experts(xb, block_expert, block_slot, block_next, n_used, w1, b1, w2, b2, tm, nb):
    def x_map(i, be, sl, nx, nu):
        return (jnp.minimum(i, nu[0] - 1), 0)

    def b_map(i, be, sl, nx, nu):
        return (be[i], 0, 0)

    grid_spec = pltpu.PrefetchScalarGridSpec(
        num_scalar_prefetch=4,
        grid=(nb,),
        in_specs=[pl.BlockSpec((tm, D_MODEL), x_map),
                  pl.BlockSpec(memory_space=pl.ANY),
                  pl.BlockSpec((1, 1, 2 * D_FF), b_map),
                  pl.BlockSpec(memory_space=pl.ANY),
                  pl.BlockSpec((1, 1, D_MODEL), b_map)],
        out_specs=pl.BlockSpec((tm, D_MODEL), lambda i, be, sl, nx, nu: (i, 0)),
        scratch_shapes=[pltpu.VMEM((2, D_MODEL, 2 * D_FF), F32),
                        pltpu.VMEM((2, D_FF, D_MODEL), F32),
                        pltpu.VMEM((D_MODEL, 2 * D_FF), BF16),
                        pltpu.VMEM((D_FF, D_MODEL), BF16),
                        pltpu.SemaphoreType.DMA((2,)), pltpu.SemaphoreType.DMA((2,))],
    )
    return pl.pallas_call(
        _expert_kernel,
        grid_spec=grid_spec,
        out_shape=jax.ShapeDtypeStruct((nb * tm, D_MODEL), F32),
        compiler_params=pltpu.CompilerParams(
            dimension_semantics=("arbitrary",), vmem_limit_bytes=VMEM_LIMIT),
        name="experts",
    )(block_expert, block_slot, block_next, n_used, xb, w1, b1, w2, b2)


def _combine_kernel(dest_ref, dnext_ref, h_ref, gate_ref, g_ref, b_ref, yb_ref, o_ref, ybuf, sem):
    i = pl.program_id(0)
    tk = h_ref.shape[0]
    slot = i % 2

    def gather(idx_ref, slot_):
        def issue(tk_i, carry):
            for kk in range(TOP_K):
                _row_copy(yb_ref, idx_ref[tk_i * TOP_K + kk], ybuf.at[slot_, kk], tk_i,
                          sem.at[slot_]).start(priority=kk % 2)
            return carry

        lax.fori_loop(0, tk, issue, 0, unroll=8)

    @pl.when(i == 0)
    def _():
        gather(dest_ref, 0)

    def drain(tk_i, carry):
        for kk in range(TOP_K):
            _row_copy(yb_ref, 0, ybuf.at[slot, kk], 0, sem.at[slot]).wait()
        return carry

    lax.fori_loop(0, tk, drain, 0, unroll=8)

    @pl.when(i + 1 < pl.num_programs(0))
    def _():
        gather(dnext_ref, 1 - slot)

    ff = jnp.zeros((tk, D_MODEL), F32)
    for kk in range(TOP_K):
        ff = ff + gate_ref[:, kk:kk + 1] * ybuf[slot, kk]
    o_ref[...] = _layer_norm(ALPHA * h_ref[...] + ff, g_ref[...], b_ref[...])


def _combine(dest_flat, h2, gates, g3, b3, yb):
    t = h2.shape[0]
    tk = min(TK_TOK, t)
    last = t // tk - 1
    return pl.pallas_call(
        _combine_kernel,
        grid=(t // tk,),
        in_specs=[pl.BlockSpec((tk * TOP_K,), lambda i: (i,), memory_space=pltpu.SMEM),
                  pl.BlockSpec((tk * TOP_K,), lambda i: (jnp.minimum(i + 1, last),),
                               memory_space=pltpu.SMEM),
                  pl.BlockSpec((tk, D_MODEL), lambda i: (i, 0)),
                  pl.BlockSpec((tk, TOP_K), lambda i: (i, 0)),
                  pl.BlockSpec((1, D_MODEL), lambda i: (0, 0)),
                  pl.BlockSpec((1, D_MODEL), lambda i: (0, 0)),
                  pl.BlockSpec(memory_space=pl.ANY)],
        out_specs=pl.BlockSpec((tk, D_MODEL), lambda i: (i, 0)),
        out_shape=jax.ShapeDtypeStruct((t, D_MODEL), F32),
        scratch_shapes=[pltpu.VMEM((2, TOP_K, tk, D_MODEL), F32), pltpu.SemaphoreType.DMA((2,))],
        compiler_params=pltpu.CompilerParams(
            dimension_semantics=("arbitrary",), vmem_limit_bytes=VMEM_LIMIT),
        name="combine",
    )(dest_flat, dest_flat, h2, gates, g3, b3, yb)


def _row(v):
    return v.astype(F32).reshape(1, -1)


def kernel(x, mem, ln_in_g, ln_in_b, w_in, hgrn_lb_logits, hgrn_norm_g, s5_lam_re, s5_lam_im,
           s5_log_dt, s5_b_re, s5_b_im, s5_c_re, s5_c_im, s5_d, s5_w_glu, s5_b_glu, w_out,
           ln1_g, ln1_b, xa_w_q, xa_w_k, xa_w_v, xa_w_o, ln2_g, ln2_b, router_w, router_b,
           exp_w1, exp_b1, exp_w2, exp_b2, ln3_g, ln3_b):
    batch, seq, _ = x.shape
    t = batch * seq
    lb_all = jnp.cumsum(jax.nn.softmax(hgrn_lb_logits.astype(F32), axis=0), axis=0)
    h = x.reshape(t, D_MODEL)
    assert DEPTH == 1 and w_in.shape[0] == 1
    for l in range(DEPTH):
        h0, qfig, u = _ln_inproj(h, _row(ln_in_g), _row(ln_in_b), w_in[l].astype(BF16))
        o_a = _hgrn2(qfig, lb_all[l].reshape(1, HG_W), _row(hgrn_norm_g[l]), batch, seq)
        s5t = _s5_tables(s5_lam_re[l], s5_lam_im[l], s5_log_dt[l], s5_b_re[l], s5_b_im[l],
                         s5_c_re[l], s5_c_im[l], s5_d[l], seq // S5_L)
        y5 = _s5(u, s5t, batch, seq)
        wkv = jnp.concatenate([xa_w_k[l], xa_w_v[l]], axis=1).astype(BF16)
        kmem, vmem = _kv_proj(mem.reshape(batch * N_MEM, D_MODEL), wkv)
        kmem = kmem.reshape(batch, N_MEM, D_MODEL)
        vmem = vmem.reshape(batch, N_MEM, D_MODEL)
        wr = jnp.pad(router_w[l].astype(F32), ((0, 0), (0, LANES - N_EXP)))
        wr3 = jnp.stack(_split3(wr)[:2], axis=0)
        br = jnp.pad(router_b[l].astype(F32), (0, LANES - N_EXP)).reshape(1, LANES)
        h2, top_idx, gates, rank, cnt = _mid(
            o_a, y5, h0, kmem, vmem, s5_w_glu[l].astype(BF16), _row(s5_b_glu[l]),
            w_out[l].astype(BF16), _row(ln1_g[l]), _row(ln1_b[l]), xa_w_q[l].astype(BF16),
            xa_w_o[l].astype(BF16), _row(ln2_g[l]), _row(ln2_b[l]), wr3, br, batch, seq)

        tm = min(TM_MOE, t)
        nb = (t * TOP_K) // tm + N_EXP
        counts = cnt[0, :N_EXP]
        padded = (counts + tm - 1) // tm * tm
        pad_end = jnp.cumsum(padded)
        pad_start = pad_end - padded
        dest = _dest_rows(pad_start.astype(jnp.int32), top_idx, rank)
        n_used = (pad_end[-1:] // tm).astype(jnp.int32)
        block_start = jnp.arange(nb, dtype=jnp.int32) * tm
        block_expert = jnp.sum(
            pad_end[None, :] <= jnp.minimum(block_start, pad_end[-1] - tm)[:, None],
            axis=1).astype(jnp.int32)
        nonempty = counts > 0
        order = jnp.cumsum(nonempty.astype(jnp.int32)) - 1
        eid = jnp.arange(N_EXP, dtype=jnp.int32)
        later = jnp.where(nonempty[None, :] & (eid[None, :] > eid[:, None]), eid[None, :], N_EXP)
        succ = jnp.min(later, axis=1)
        succ = jnp.where(succ < N_EXP, succ, -1).astype(jnp.int32)
        block_slot = (order[block_expert] % 2).astype(jnp.int32)
        block_next = succ[block_expert]
        xb = _dispatch(h2, dest, (pad_start + counts).astype(jnp.int32),
                       (padded - counts).astype(jnp.int32), n_used, tm, nb)
        yb = _experts(xb, block_expert, block_slot, block_next, n_used, exp_w1[l],
                      exp_b1[l].reshape(N_EXP, 1, 2 * D_FF), exp_w2[l],
                      exp_b2[l].reshape(N_EXP, 1, D_MODEL), tm, nb)
        h = _combine(dest, h2, gates, _row(ln3_g[l]), _row(ln3_b[l]), yb)
    return h.reshape(batch, seq, D_MODEL)
```

```python
import functools
import math

import jax
import jax.numpy as jnp
from jax import lax
from jax.experimental import pallas as pl
from jax.experimental.pallas import tpu as pltpu

F32 = jnp.float32
BF16 = jnp.bfloat16

D_MODEL = 1024
HG_W = 512
HG_D = 128
HG_H = HG_W // HG_D
HG_CHUNK = 64
HG_SUB = 16
S5_W = 512
S5_N = 16
S5_G = S5_W // S5_N
S5_P = 64
S5_L = 64
S5_TB = 8
IN_COLS = 4 * HG_W + S5_W
N_MEM = 256
XA_H = 4
XA_D = D_MODEL // XA_H
XA_SCALE = XA_D ** -0.5
N_EXP = 32
TOP_K = 4
D_FF = D_MODEL
SWIGLU_LIMIT = 7.0
SWIGLU_ALPHA = 1.702
DEPTH = 1
ALPHA = (2 * DEPTH) ** 0.25
LN_EPS = 1e-5
RMS_EPS = 1e-6
LANES = 128

TM_IN = 512
TQ_MID = 512
TM_MOE = 256
TK_TOK = 256
VMEM_LIMIT = 56 * 1024 * 1024


def _layer_norm(x, g, b):
    mu = jnp.mean(x, axis=-1, keepdims=True)
    xc = x - mu
    var = jnp.mean(xc * xc, axis=-1, keepdims=True)
    return xc * lax.rsqrt(var + LN_EPS) * g + b


def _dot(a, b):
    return jnp.dot(a, b, preferred_element_type=F32)


def _dot_nt(a, b):
    return lax.dot_general(a, b, (((1,), (1,)), ((), ())), preferred_element_type=F32)


def _dot_tn(a, b):
    return lax.dot_general(a, b, (((0,), (0,)), ((), ())), preferred_element_type=F32)


def _split3(x):
    hi = x.astype(BF16)
    r1 = x - hi.astype(F32)
    mid = r1.astype(BF16)
    lo = (r1 - mid.astype(F32)).astype(BF16)
    return hi, mid, lo


def _ln_inproj_kernel(x_ref, g_ref, b_ref, w_ref, h_ref, qfig_ref, u_ref):
    h = _layer_norm(x_ref[...], g_ref[...], b_ref[...])
    h_ref[...] = h
    p = _dot(h.astype(BF16), w_ref[...])
    qfig_ref[...] = p[:, :4 * HG_W]
    u_ref[...] = p[:, 4 * HG_W:].astype(u_ref.dtype)


def _ln_inproj(x2, g, b, w_bf):
    t = x2.shape[0]
    tm = min(TM_IN, t)
    return pl.pallas_call(
        _ln_inproj_kernel,
        grid=(t // tm,),
        in_specs=[
            pl.BlockSpec((tm, D_MODEL), lambda i: (i, 0)),
            pl.BlockSpec((1, D_MODEL), lambda i: (0, 0)),
            pl.BlockSpec((1, D_MODEL), lambda i: (0, 0)),
            pl.BlockSpec((D_MODEL, IN_COLS), lambda i: (0, 0)),
        ],
        out_specs=[
            pl.BlockSpec((tm, D_MODEL), lambda i: (i, 0)),
            pl.BlockSpec((tm, 4 * HG_W), lambda i: (i, 0)),
            pl.BlockSpec((tm, S5_W), lambda i: (i, 0)),
        ],
        out_shape=[
            jax.ShapeDtypeStruct((t, D_MODEL), F32),
            jax.ShapeDtypeStruct((t, 4 * HG_W), F32),
            jax.ShapeDtypeStruct((t, S5_W), BF16),
        ],
        compiler_params=pltpu.CompilerParams(
            dimension_semantics=("arbitrary",), vmem_limit_bytes=VMEM_LIMIT),
        name="ln_inproj",
    )(x2, g, b, w_bf)


def _hgrn_kernel(q_ref, f_ref, i_ref, g_ref, lb_ref, ng_ref, o_ref, st_ref, k_s, cum_s):
    nseq = q_ref.shape[0]
    rows = nseq * HG_CHUNK

    @pl.when(pl.program_id(0) == 0)
    def _():
        st_ref[...] = jnp.zeros_like(st_ref)

    lb = lb_ref[...]
    f = lb + (1.0 - lb) * jax.nn.sigmoid(f_ref[...].reshape(rows, HG_W))
    k_s[...] = 1.0 - f
    r = lax.broadcasted_iota(jnp.int32, (rows, rows), 0)
    c = lax.broadcasted_iota(jnp.int32, (rows, rows), 1)
    tri = ((c <= r) & ((r // HG_CHUNK) == (c // HG_CHUNK))).astype(BF16)
    hi, mid, lo = _split3(jnp.log2(f))
    cum_s[...] = _dot(tri, hi) + _dot(tri, mid) + _dot(tri, lo)

    row_in_sub = lax.broadcasted_iota(jnp.int32, (HG_SUB, 1), 0)
    ng = ng_ref[...]
    for b in range(nseq):
        r0 = b * HG_CHUNK
        for h in range(HG_H):
            cols = slice(h * HG_D, (h + 1) * HG_D)
            qc = q_ref[b, :, cols]
            kc = k_s[r0:r0 + HG_CHUNK, cols]
            vc = i_ref[b, :, cols]
            cc = cum_s[r0:r0 + HG_CHUNK, cols]
            last = cum_s[r0 + HG_CHUNK - 1:r0 + HG_CHUNK, cols]
            st = st_ref[b, h]
            inter = _dot_nt((qc * jnp.exp2(cc)).astype(BF16), st.astype(BF16))
            vb = vc.astype(BF16)
            outs = []
            for blk in range(HG_CHUNK // HG_SUB):
                rs = slice(HG_SUB * blk, HG_SUB * (blk + 1))
                q_i = qc[rs]
                c_i = cc[rs]
                acc = inter[rs]
                if blk > 0:
                    n = HG_SUB * blk
                    c_ref = cum_s[r0 + n - 1:r0 + n, cols]
                    khat = (kc[:n] * jnp.exp2(c_ref - cc[:n])).astype(BF16)
                    qhat = (q_i * jnp.exp2(c_i - c_ref)).astype(BF16)
                    acc = acc + _dot(_dot_nt(qhat, khat).astype(BF16), vb[:n])
                for jj in range(HG_SUB):
                    srow = HG_SUB * blk + jj
                    c_s = cum_s[r0 + srow:r0 + srow + 1, cols]
                    k_row = k_s[r0 + srow:r0 + srow + 1, cols]
                    v_row = i_ref[b, srow:srow + 1, cols]
                    p = q_i * k_row * jnp.exp2(c_i - c_s)
                    w = jnp.sum(p, axis=-1, keepdims=True)
                    w = jnp.where(row_in_sub >= jj, w, 0.0)
                    acc = acc + w * v_row
                outs.append(acc)
            o = jnp.concatenate(outs, axis=0)
            kd = (kc * jnp.exp2(last - cc)).astype(BF16)
            st_ref[b, h] = st * jnp.exp2(last) + _dot_tn(vb, kd)
            o = o * lax.rsqrt(jnp.mean(o * o, axis=-1, keepdims=True) + RMS_EPS) * ng
            gg = g_ref[b, :, cols]
            o_ref[b, :, cols] = (o * (gg * jax.nn.sigmoid(gg))).astype(o_ref.dtype)


def _hgrn2(qfig, lb, ng, batch, seq):
    nj = seq // HG_CHUNK
    qfig3 = qfig.reshape(batch, seq, 4 * HG_W)

    def col_spec(cb):
        return pl.BlockSpec((batch, HG_CHUNK, HG_W), lambda j: (0, j, cb))

    out = pl.pallas_call(
        _hgrn_kernel,
        grid=(nj,),
        in_specs=[col_spec(0), col_spec(1), col_spec(2), col_spec(3),
                  pl.BlockSpec((1, HG_W), lambda j: (0, 0)),
                  pl.BlockSpec((1, HG_D), lambda j: (0, 0))],
        out_specs=pl.BlockSpec((batch, HG_CHUNK, HG_W), lambda j: (0, j, 0)),
        out_shape=jax.ShapeDtypeStruct((batch, seq, HG_W), BF16),
        scratch_shapes=[pltpu.VMEM((batch, HG_H, HG_D, HG_D), F32),
                        pltpu.VMEM((batch * HG_CHUNK, HG_W), F32),
                        pltpu.VMEM((batch * HG_CHUNK, HG_W), F32)],
        compiler_params=pltpu.CompilerParams(
            dimension_semantics=("arbitrary",), vmem_limit_bytes=VMEM_LIMIT),
        name="hgrn2",
    )(qfig3, qfig3, qfig3, qfig3, lb, ng)
    return out.reshape(batch * seq, HG_W)


def _s5_kernel(u_ref, k2_ref, hm_ref, gm_ref, ab_ref, d_ref, y_ref, *, nc):
    ub = u_ref[0]
    rows = ub.shape[0]
    nblk = S5_L // S5_TB
    blocks = []
    for dlt in reversed(range(nblk)):
        for s in range(S5_TB):
            off = S5_N * (S5_TB * dlt - s + S5_TB - 1)
            base = off // LANES * LANES
            win = k2_ref[0, :, base:base + 2 * LANES]
            if off != base:
                win = pltpu.roll(win, 2 * LANES - (off - base), axis=1)
            blocks.append(win[:, :LANES].astype(BF16))
    tt_rev = jnp.concatenate(blocks, axis=0)
    ys = []
    for bc in range(nblk):
        ys.append(_dot(ub[:, :(bc + 1) * LANES], tt_rev[(nblk - 1 - bc) * LANES:, :]))
    y = jnp.concatenate(ys, axis=1)
    x = _dot(ub, hm_ref[0])
    chunk_id = lax.broadcasted_iota(jnp.int32, (rows, 2 * S5_P), 0) % nc
    lev = 0
    while (1 << lev) < nc:
        sh = 1 << lev
        a = ab_ref[0, 2 * lev:2 * lev + 1, :]
        bv = ab_ref[0, 2 * lev + 1:2 * lev + 2, :]
        xs = jnp.where(chunk_id >= sh, pltpu.roll(x, sh, axis=0), 0.0)
        x = x + xs * a + pltpu.roll(xs, S5_P, axis=1) * bv
        lev += 1
    prev = jnp.where(chunk_id >= 1, pltpu.roll(x, 1, axis=0), 0.0)
    y = y + _dot(prev.astype(BF16), gm_ref[0])
    y = y + d_ref[0] * ub.astype(F32)
    y_ref[0] = jax.nn.gelu(y).astype(y_ref.dtype)


def _s5_tables(lam_re, lam_im, log_dt, b_re, b_im, c_re, c_im, d_skip, nc):
    hp = lax.Precision.HIGHEST
    lam_r, lam_i = lam_re.astype(F32), lam_im.astype(F32)
    dt = jnp.exp(log_dt.astype(F32))
    z_r, z_i = lam_r * dt, lam_i * dt

    def lam_pow(zr, zi, tau):
        mag = jnp.exp(zr * tau)
        return mag * jnp.cos(zi * tau), mag * jnp.sin(zi * tau)

    taus = jnp.arange(S5_L + 1, dtype=F32)[None, :, None]
    p_r, p_i = lam_pow(z_r[:, None, :], z_i[:, None, :], taus)
    x_r, x_i = p_r[:, 1] - 1.0, p_i[:, 1]
    den = lam_r * lam_r + lam_i * lam_i
    f_r = ((x_r * lam_r + x_i * lam_i) / den)[..., None]
    f_i = ((x_i * lam_r - x_r * lam_i) / den)[..., None]
    b_r, b_i = b_re.astype(F32), b_im.astype(F32)
    bb_r = f_r * b_r - f_i * b_i
    bb_i = f_r * b_i + f_i * b_r
    lane = jnp.arange(S5_L * S5_N)
    sel_t = (lane[None, :] // S5_N == jnp.arange(S5_L)[:, None]).astype(F32)
    sel_n = (lane[None, :] % S5_N == jnp.arange(S5_N)[:, None]).astype(F32)
    ce_r = jnp.einsum('gnp,nl->gpl', c_re.astype(F32), sel_n, precision=hp)
    ce_i = jnp.einsum('gnp,nl->gpl', c_im.astype(F32), sel_n, precision=hp)
    t_r = jnp.einsum('gtp,tl->gpl', p_r[:, :S5_L], sel_t, precision=hp)
    t_i = jnp.einsum('gtp,tl->gpl', p_i[:, :S5_L], sel_t, precision=hp)
    k2 = (jnp.einsum('gpm,gpl->gml', bb_r, ce_r * t_r - ce_i * t_i, precision=hp)
          - jnp.einsum('gpm,gpl->gml', bb_i, ce_r * t_i + ce_i * t_r, precision=hp))
    lead = (S5_TB - 1) * S5_N
    k2 = jnp.pad(k2, ((0, 0), (0, 0), (lead, LANES - lead)))
    q_r = p_r[:, S5_L - 1::-1][:, :, None, :]
    q_i = p_i[:, S5_L - 1::-1][:, :, None, :]
    bt_r = bb_r.transpose(0, 2, 1)[:, None]
    bt_i = bb_i.transpose(0, 2, 1)[:, None]
    hm = jnp.concatenate([q_r * bt_r - q_i * bt_i, q_r * bt_i + q_i * bt_r], axis=-1)
    hm = hm.reshape(S5_G, S5_L * S5_N, 2 * S5_P)
    e_r = jnp.einsum('gtp,tl->gpl', p_r[:, 1:], sel_t, precision=hp)
    e_i = jnp.einsum('gtp,tl->gpl', p_i[:, 1:], sel_t, precision=hp)
    gm =jnp.concatenate([ce_r * e_r - ce_i * e_i, -(ce_r * e_i + ce_i * e_r)], axis=1)
    rows = []
    lev = 0
    while (1 << lev) < nc:
        a, b = lam_pow(z_r, z_i, float(S5_L * (1 << lev)))
        rows.append(jnp.concatenate([a, a], axis=-1))
        rows.append(jnp.concatenate([-b, b], axis=-1))
        lev += 1
    if not rows:
        rows = [jnp.zeros((S5_G, 2 * S5_P), F32)] * 2
    ab = jnp.stack(rows, axis=1)
    dvec = jnp.einsum('gn,nl->gl', d_skip.astype(F32).reshape(S5_G, S5_N), sel_n, precision=hp)
    return k2, hm.astype(BF16), gm.astype(BF16), ab, dvec.reshape(S5_G, 1, S5_L * S5_N)


def _s5(u, tables, batch, seq):
    k2, hm, gm, ab, dvec = tables
    nc = seq // S5_L
    rows = batch * nc
    width = S5_L * S5_N
    ug = u.reshape(batch, nc, S5_L, S5_G, S5_N).transpose(3, 0, 1, 2, 4).reshape(S5_G, rows, width)
    nab = ab.shape[1]
    yg = pl.pallas_call(
        functools.partial(_s5_kernel, nc=nc),
        grid=(S5_G,),
        in_specs=[
            pl.BlockSpec((1, rows, width), lambda g: (g, 0, 0)),
            pl.BlockSpec((1, S5_N, k2.shape[2]), lambda g: (g, 0, 0)),
            pl.BlockSpec((1, width, 2 * S5_P), lambda g: (g, 0, 0)),
            pl.BlockSpec((1, 2 * S5_P, width), lambda g: (g, 0, 0)),
            pl.BlockSpec((1, nab, 2 * S5_P), lambda g: (g, 0, 0)),
            pl.BlockSpec((1, 1, width), lambda g: (g, 0, 0)),
        ],
        out_specs=pl.BlockSpec((1, rows, width), lambda g: (g, 0, 0)),
        out_shape=jax.ShapeDtypeStruct((S5_G, rows, width), BF16),
        compiler_params=pltpu.CompilerParams(
            dimension_semantics=("arbitrary",), vmem_limit_bytes=VMEM_LIMIT),
        name="s5",
    )(ug, k2, hm, gm, ab, dvec)
    return yg.reshape(S5_G, batch, nc, S5_L, S5_N).transpose(1, 2, 3, 0, 4).reshape(
        batch * seq, S5_W)


def _kv_kernel(m_ref, w_ref, k_ref, v_ref):
    kv = _dot(m_ref[...].astype(BF16), w_ref[...])
    k_ref[...] = kv[:, :D_MODEL].astype(BF16)
    v_ref[...] = kv[:, D_MODEL:].astype(BF16)


def _kv_proj(mem2, wkv_bf):
    rows = mem2.shape[0]
    tm = min(256, rows)
    return pl.pallas_call(
        _kv_kernel,
        grid=(rows // tm,),
        in_specs=[pl.BlockSpec((tm, D_MODEL), lambda i: (i, 0)),
                  pl.BlockSpec((D_MODEL, 2 * D_MODEL), lambda i: (0, 0))],
        out_specs=[pl.BlockSpec((tm, D_MODEL), lambda i: (i, 0)),
                   pl.BlockSpec((tm, D_MODEL), lambda i: (i, 0))],
        out_shape=[jax.ShapeDtypeStruct((rows, D_MODEL), BF16),
                   jax.ShapeDtypeStruct((rows, D_MODEL), BF16)],
        compiler_params=pltpu.CompilerParams(
            dimension_semantics=("arbitrary",), vmem_limit_bytes=VMEM_LIMIT),
        name="kv_proj",
    )(mem2, wkv_bf)


def _mid_kernel(oa_ref, y5_ref, h0_ref, k_ref, v_ref, wglu_ref, bglu_ref, wout_ref,
                g1_ref, b1_ref, wq_ref, wo_ref, g2_ref, b2_ref, wr_ref, br_ref,
                h2_ref, idx_ref, gate_ref, rank_ref, cnt_ref, run_ref):
    tq = oa_ref.shape[0]

    @pl.when((pl.program_id(0) == 0) & (pl.program_id(1) == 0))
    def _():
        run_ref[...] = jnp.zeros_like(run_ref)

    y5b = y5_ref[...]
    ob = y5b.astype(F32) * jax.nn.sigmoid(_dot(y5b, wglu_ref[...]) + bglu_ref[...])
    mix = _dot(oa_ref[...], wout_ref[:HG_W, :]) + _dot(ob.astype(BF16), wout_ref[HG_W:, :])
    h1 = _layer_norm(ALPHA * h0_ref[...] + mix, g1_ref[...], b1_ref[...])

    q = _dot(h1.astype(BF16), wq_ref[...])
    heads = []
    for hd in range(XA_H):
        cs = slice(hd * XA_D, (hd + 1) * XA_D)
        s = _dot_nt(q[:, cs].astype(BF16), k_ref[0, :, cs]) * XA_SCALE
        e = jnp.exp(s - jnp.max(s, axis=-1, keepdims=True))
        p = e / jnp.sum(e, axis=-1, keepdims=True)
        heads.append(_dot(p.astype(BF16), v_ref[0, :, cs]))
    xa = _dot(jnp.concatenate(heads, axis=1).astype(BF16), wo_ref[...])
    h2 = _layer_norm(ALPHA * h1 + xa, g2_ref[...], b2_ref[...])
    h2_ref[...] = h2

    h_hi, h_mid, _ = _split3(h2)
    w_hi, w_mid = wr_ref[0], wr_ref[1]
    logits = (_dot(h_hi, w_hi) + _dot(h_mid, w_hi) + _dot(h_hi, w_mid)) + br_ref[...]
    lane = lax.broadcasted_iota(jnp.int32, (tq, LANES), 1)
    lane_f = lane.astype(F32)
    work = jnp.where(lane < N_EXP, logits, -jnp.inf)
    vals, onehots = [], []
    for kk in range(TOP_K):
        mx = jnp.max(work, axis=-1, keepdims=True)
        sel = jnp.min(jnp.where(work == mx, lane_f, float(LANES)), axis=-1, keepdims=True)
        oh = lane_f == sel
        idx_ref[:, kk:kk + 1] = sel.astype(jnp.int32)
        vals.append(mx)
        onehots.append(oh)
        work = jnp.where(oh, -jnp.inf, work)
    es = [jnp.exp(v - vals[0]) for v in vals]
    den = es[0] + es[1] + es[2] + es[3]
    for kk in range(TOP_K):
        gate_ref[:, kk:kk + 1] = es[kk] / den

    member = jnp.zeros((tq, LANES), F32)
    for oh in onehots:
        member = member + oh.astype(F32)
    r = lax.broadcasted_iota(jnp.int32, (tq, tq), 0)
    c = lax.broadcasted_iota(jnp.int32, (tq, tq), 1)
    before = _dot((c < r).astype(BF16), member.astype(BF16)) + run_ref[...]
    for kk in range(TOP_K):
        rk = jnp.sum(jnp.where(onehots[kk], before, 0.0), axis=-1, keepdims=True)
        rank_ref[:, kk:kk + 1] = rk.astype(jnp.int32)
    run = run_ref[...] + jnp.sum(member, axis=0, keepdims=True)
    run_ref[...] = run
    cnt_ref[...] = run.astype(jnp.int32)


def _mid(oa, y5, h0, kmem, vmem, wglu, bglu, wout, g1, b1, wq, wo, g2, b2, wr3, br, batch, seq):
    t = oa.shape[0]
    tq = min(TQ_MID, seq)
    nj = seq // tq

    def row_spec(w):
        return pl.BlockSpec((tq, w), lambda b, j: (b * nj + j, 0))

    def full_spec(shape):
        nd = len(shape)
        return pl.BlockSpec(shape, lambda b, j: (0,) * nd)

    mem_spec = pl.BlockSpec((1, N_MEM, D_MODEL), lambda b, j: (b, 0, 0))
    return pl.pallas_call(
        _mid_kernel,
        grid=(batch, nj),
        in_specs=[row_spec(HG_W), row_spec(S5_W), row_spec(D_MODEL), mem_spec, mem_spec,
                  full_spec((S5_W, S5_W)), full_spec((1, S5_W)),
                  full_spec((D_MODEL, D_MODEL)), full_spec((1, D_MODEL)), full_spec((1, D_MODEL)),
                  full_spec((D_MODEL, D_MODEL)), full_spec((D_MODEL, D_MODEL)),
                  full_spec((1, D_MODEL)), full_spec((1, D_MODEL)),
                  full_spec((2, D_MODEL, LANES)), full_spec((1, LANES))],
        out_specs=[row_spec(D_MODEL), row_spec(TOP_K), row_spec(TOP_K), row_spec(TOP_K),
                   full_spec((1, LANES))],
        out_shape=[jax.ShapeDtypeStruct((t, D_MODEL), F32),
                   jax.ShapeDtypeStruct((t, TOP_K), jnp.int32),
                   jax.ShapeDtypeStruct((t, TOP_K), F32),
                   jax.ShapeDtypeStruct((t, TOP_K), jnp.int32),
                   jax.ShapeDtypeStruct((1, LANES), jnp.int32)],
        scratch_shapes=[pltpu.VMEM((1, LANES), F32)],
        compiler_params=pltpu.CompilerParams(
            dimension_semantics=("arbitrary", "arbitrary"), vmem_limit_bytes=VMEM_LIMIT),
        name="mid",
    )(oa, y5, h0, kmem, vmem, wglu, bglu, wout, g1, b1, wq, wo, g2, b2, wr3, br)


def _dest_kernel(ps_ref, idx_ref, rank_ref, o_ref):
    idx = idx_ref[...]
    acc = rank_ref[...]
    for e in range(N_EXP):
        acc = acc + jnp.where(idx == e, ps_ref[e], 0)
    o_ref[...] = acc


def _dest_rows(pad_start, top_idx, rank):
    n = top_idx.size
    shape = (n // LANES, LANES)
    grid_spec = pltpu.PrefetchScalarGridSpec(
        num_scalar_prefetch=1,
        grid=(1,),
        in_specs=[pl.BlockSpec(shape, lambda i, ps: (0, 0)),
                  pl.BlockSpec(shape, lambda i, ps: (0, 0))],
        out_specs=pl.BlockSpec(shape, lambda i, ps: (0, 0)),
    )
    dest = pl.pallas_call(
        _dest_kernel,
        grid_spec=grid_spec,
        out_shape=jax.ShapeDtypeStruct(shape, jnp.int32),
        name="dest_rows",
    )(pad_start, top_idx.reshape(shape), rank.reshape(shape))
    return dest.reshape(n)


def _row_copy(src, src_row, dst, dst_row, sem):
    return pltpu.make_async_copy(src.at[pl.ds(src_row, 1)], dst.at[pl.ds(dst_row, 1)], sem)


def _dispatch_kernel(pz_ref, pc_ref, nu_ref, dest_ref, h_ref, xb_ref, zbuf, sem, zsem, *, tm, nb):
    i = pl.program_id(0)
    tk = h_ref.shape[0]
    sub = 8
    nbits = (tm // sub).bit_length() - 1

    def zero_fill(start_not_wait):
        def act(cp):
            if start_not_wait:
                cp.start()
            else:
                cp.wait()

        def per_expert(e, carry):
            cnt = pc_ref[e]
            z0 = pz_ref[e]
            head = (-z0) & (sub - 1)
            for rr in range(sub - 1):
                @pl.when(rr < head)
                def _():
                    act(_row_copy(zbuf, 0, xb_ref, z0 + rr, zsem))
            za = z0 + head
            n_tiles = (cnt - head) >> 3
            for bit in range(nbits):
                size = sub << bit

                @pl.when(((n_tiles >> bit) & 1) == 1)
                def _():
                    off = pl.multiple_of(za + sub * (n_tiles & ((1 << bit) - 1)), sub)
                    act(pltpu.make_async_copy(zbuf.at[pl.ds(0, size)],
                                              xb_ref.at[pl.ds(off, size)], zsem))
            return carry

        lax.fori_loop(0, N_EXP, per_expert, 0)
        for jb in range(N_EXP):
            blk = nu_ref[0] + jb

            @pl.when(blk < nb)
            def _():
                off = pl.multiple_of(blk * tm, tm)
                act(pltpu.make_async_copy(zbuf, xb_ref.at[pl.ds(off, tm)], zsem))

    @pl.when(i == 0)
    def _():
        zbuf[...] = jnp.zeros_like(zbuf)
        zero_fill(True)

    def issue(tk_i, carry):
        for kk in range(TOP_K):
            _row_copy(h_ref, tk_i, xb_ref, dest_ref[tk_i * TOP_K + kk], sem).start(
                priority=kk % 2)
        return carry

    lax.fori_loop(0, tk, issue, 0, unroll=8)

    def drain(tk_i, carry):
        for kk in range(TOP_K):
            _row_copy(h_ref, 0, xb_ref, 0, sem).wait()
        return carry

    lax.fori_loop(0, tk, drain, 0, unroll=8)

    @pl.when(i == pl.num_programs(0) - 1)
    def _():
        zero_fill(False)


def _dispatch(h2, dest_flat, pad_zero_start, pad_cnt, n_used, tm, nb):
    t = h2.shape[0]
    tk = min(TK_TOK, t)
    grid_spec = pltpu.PrefetchScalarGridSpec(
        num_scalar_prefetch=3,
        grid=(t // tk,),
        in_specs=[pl.BlockSpec((tk * TOP_K,), lambda i, *_: (i,), memory_space=pltpu.SMEM),
                  pl.BlockSpec((tk, D_MODEL), lambda i, *_: (i, 0))],
        out_specs=pl.BlockSpec(memory_space=pl.ANY),
        scratch_shapes=[pltpu.VMEM((tm, D_MODEL), F32),
                        pltpu.SemaphoreType.DMA(()), pltpu.SemaphoreType.DMA(())],
    )
    return pl.pallas_call(
        functools.partial(_dispatch_kernel, tm=tm, nb=nb),
        grid_spec=grid_spec,
        out_shape=jax.ShapeDtypeStruct((nb * tm, D_MODEL), F32),
        compiler_params=pltpu.CompilerParams(
            dimension_semantics=("arbitrary",), vmem_limit_bytes=VMEM_LIMIT,
            has_side_effects=True),
        name="dispatch",
    )(pad_zero_start, pad_cnt, n_used, dest_flat, h2)


def _expert_kernel(be_ref, slot_ref, nxt_ref, nu_ref, xb_ref, w1_ref, b1_ref, w2_ref, b2_ref,
                   yb_ref, w1f, w2f, w1b, w2b, sem1, sem2):
    i = pl.program_id(0)
    used = i < nu_ref[0]
    e = be_ref[i]
    slot = slot_ref[e]
    new_expert = (i == 0) | (e != be_ref[jnp.maximum(i - 1, 0)])

    def fetch(expert, slot_):
        return (pltpu.make_async_copy(w1_ref.at[expert], w1f.at[slot_], sem1.at[slot_]),
                pltpu.make_async_copy(w2_ref.at[expert], w2f.at[slot_], sem2.at[slot_]))

    @pl.when(used & (i == 0))
    def _():
        for cp in fetch(e, slot):
            cp.start()

    @pl.when(used & new_expert)
    def _():
        for cp in fetch(e, slot):
            cp.wait()
        w1b[...] = w1f[slot].astype(BF16)
        w2b[...] = w2f[slot].astype(BF16)
        nxt = nxt_ref[e]

        @pl.when(nxt >= 0)
        def _():
            for cp in fetch(nxt, 1 - slot):
                cp.start()

    @pl.when(used)
    def _():
        x = xb_ref[...].astype(BF16)
        hh = _dot(x, w1b[...]) + b1_ref[0]
        x_glu = jnp.minimum(hh[:, :D_FF], SWIGLU_LIMIT)
        x_lin = jnp.clip(hh[:, D_FF:], -SWIGLU_LIMIT, SWIGLU_LIMIT)
        act = x_glu * jax.nn.sigmoid(SWIGLU_ALPHA * x_glu) * (x_lin + 1.0)
        yb_ref[...] = _dot(act.astype(BF16), w2b[...]) + b2_ref[0]

    @pl.when(i >= nu_ref[0])
    def _():
        yb_ref[...] = jnp.zeros_like(yb_ref)


def _experts(xb, block_expert, expert_slot, expert_next, n_used, w1, b1, w2, b2, tm, nb):
    def x_map(i, be, sl, nx, nu):
        return (jnp.minimum(i, nu[0] - 1), 0)

    def b_map(i, be, sl, nx, nu):
        return (be[i], 0, 0)

    grid_spec = pltpu.PrefetchScalarGridSpec(
        num_scalar_prefetch=4,
        grid=(nb,),
        in_specs=[pl.BlockSpec((tm, D_MODEL), x_map),
                  pl.BlockSpec(memory_space=pl.ANY),
                  pl.BlockSpec((1, 1, 2 * D_FF), b_map),
                  pl.BlockSpec(memory_space=pl.ANY),
                  pl.BlockSpec((1, 1, D_MODEL), b_map)],
        out_specs=pl.BlockSpec((tm, D_MODEL), lambda i, be, sl, nx, nu: (i, 0)),
        scratch_shapes=[pltpu.VMEM((2, D_MODEL, 2 * D_FF), F32),
                        pltpu.VMEM((2, D_FF, D_MODEL), F32),
                        pltpu.VMEM((D_MODEL, 2 * D_FF), BF16),
                        pltpu.VMEM((D_FF, D_MODEL), BF16),
                        pltpu.SemaphoreType.DMA((2,)), pltpu.SemaphoreType.DMA((2,))],
    )
    return pl.pallas_call(
        _expert_kernel,
        grid_spec=grid_spec,
        out_shape=jax.ShapeDtypeStruct((nb * tm, D_MODEL), F32),
        compiler_params=pltpu.CompilerParams(
            dimension_semantics=("arbitrary",), vmem_limit_bytes=VMEM_LIMIT),
        name="experts",
    )(block_expert, expert_slot, expert_next, n_used, xb, w1, b1, w2, b2)


def _combine_kernel(dest_ref, dnext_ref, h_ref, gate_ref, g_ref, b_ref, yb_ref, o_ref, ybuf, sem):
    i = pl.program_id(0)
    tk = h_ref.shape[0]
    slot = i % 2

    def gather(idx_ref, slot_):
        def issue(tk_i, carry):
            for kk in range(TOP_K):
                _row_copy(yb_ref, idx_ref[tk_i * TOP_K + kk], ybuf.at[slot_, kk], tk_i,
                          sem.at[slot_]).start(priority=kk % 2)
            return carry

        lax.fori_loop(0, tk, issue, 0, unroll=8)

    @pl.when(i == 0)
    def _():
        gather(dest_ref, 0)

    def drain(tk_i, carry):
        for kk in range(TOP_K):
            _row_copy(yb_ref, 0, ybuf.at[slot, kk], 0, sem.at[slot]).wait()
        return carry

    lax.fori_loop(0, tk, drain, 0, unroll=8)

    @pl.when(i + 1 < pl.num_programs(0))
    def _():
        gather(dnext_ref, 1 - slot)

    ff = jnp.zeros((tk, D_MODEL), F32)
    for kk in range(TOP_K):
        ff = ff + gate_ref[:, kk:kk + 1] * ybuf[slot, kk]
    o_ref[...] = _layer_norm(ALPHA * h_ref[...] + ff, g_ref[...], b_ref[...])


def _combine(dest_flat, h2, gates, g3, b3, yb):
    t = h2.shape[0]
    tk = min(TK_TOK, t)
    last = t // tk - 1
    return pl.pallas_call(
        _combine_kernel,
        grid=(t // tk,),
        in_specs=[pl.BlockSpec((tk * TOP_K,), lambda i: (i,), memory_space=pltpu.SMEM),
                  pl.BlockSpec((tk * TOP_K,), lambda i: (jnp.minimum(i + 1, last),),
                               memory_space=pltpu.SMEM),
                  pl.BlockSpec((tk, D_MODEL), lambda i: (i, 0)),
                  pl.BlockSpec((tk, TOP_K), lambda i: (i, 0)),
                  pl.BlockSpec((1, D_MODEL), lambda i: (0, 0)),
                  pl.BlockSpec((1, D_MODEL), lambda i: (0, 0)),
                  pl.BlockSpec(memory_space=pl.ANY)],
        out_specs=pl.BlockSpec((tk, D_MODEL), lambda i: (i, 0)),
        out_shape=jax.ShapeDtypeStruct((t, D_MODEL), F32),
        scratch_shapes=[pltpu.VMEM((2, TOP_K, tk, D_MODEL), F32), pltpu.SemaphoreType.DMA((2,))],
        compiler_params=pltpu.CompilerParams(
            dimension_semantics=("arbitrary",), vmem_limit_bytes=VMEM_LIMIT),
        name="combine",
    )(dest_flat, dest_flat, h2, gates, g3, b3, yb)


def _row(v):
    return v.astype(F32).reshape(1, -1)


def kernel(x, mem, ln_in_g, ln_in_b, w_in, hgrn_lb_logits, hgrn_norm_g, s5_lam_re, s5_lam_im,
           s5_log_dt, s5_b_re, s5_b_im, s5_c_re, s5_c_im, s5_d, s5_w_glu, s5_b_glu, w_out,
           ln1_g, ln1_b, xa_w_q, xa_w_k, xa_w_v, xa_w_o, ln2_g, ln2_b, router_w, router_b,
           exp_w1, exp_b1, exp_w2, exp_b2, ln3_g, ln3_b):
    batch, seq, _ = x.shape
    t = batch * seq
    lb_all = jnp.cumsum(jax.nn.softmax(hgrn_lb_logits.astype(F32), axis=0), axis=0)
    h = x.reshape(t, D_MODEL)
    assert DEPTH == 1 and w_in.shape[0] == 1
    for l in range(DEPTH):
        h0, qfig, u = _ln_inproj(h, _row(ln_in_g), _row(ln_in_b), w_in[l].astype(BF16))
        o_a = _hgrn2(qfig, lb_all[l].reshape(1, HG_W), _row(hgrn_norm_g[l]), batch, seq)
        s5t = _s5_tables(s5_lam_re[l], s5_lam_im[l], s5_log_dt[l], s5_b_re[l], s5_b_im[l],
                         s5_c_re[l], s5_c_im[l], s5_d[l], seq // S5_L)
        y5 = _s5(u, s5t, batch, seq)
        wkv = jnp.concatenate([xa_w_k[l], xa_w_v[l]], axis=1).astype(BF16)
        kmem, vmem = _kv_proj(mem.reshape(batch * N_MEM, D_MODEL), wkv)
        kmem = kmem.reshape(batch, N_MEM, D_MODEL)
        vmem = vmem.reshape(batch, N_MEM, D_MODEL)
        wr = jnp.pad(router_w[l].astype(F32), ((0, 0), (0, LANES - N_EXP)))
        wr3 = jnp.stack(_split3(wr)[:2], axis=0)
        br = jnp.pad(router_b[l].astype(F32), (0, LANES - N_EXP)).reshape(1, LANES)
        h2, top_idx, gates, rank, cnt = _mid(
            o_a, y5, h0, kmem, vmem, s5_w_glu[l].astype(BF16), _row(s5_b_glu[l]),
            w_out[l].astype(BF16), _row(ln1_g[l]), _row(ln1_b[l]), xa_w_q[l].astype(BF16),
            xa_w_o[l].astype(BF16), _row(ln2_g[l]), _row(ln2_b[l]), wr3, br, batch, seq)

        tm = min(TM_MOE, t)
        nb = (t * TOP_K) // tm + N_EXP
        counts = cnt[0, :N_EXP]
        padded = (counts + tm - 1) // tm * tm
        pad_end = jnp.cumsum(padded)
        pad_start = pad_end - padded
        dest = _dest_rows(pad_start.astype(jnp.int32), top_idx, rank)
        n_used = (pad_end[-1:] // tm).astype(jnp.int32)
        block_start = jnp.arange(nb, dtype=jnp.int32) * tm
        block_expert = jnp.sum(
            pad_end[None, :] <= jnp.minimum(block_start, pad_end[-1] - tm)[:, None],
            axis=1).astype(jnp.int32)
        nonempty = counts > 0
        order = jnp.cumsum(nonempty.astype(jnp.int32)) - 1
        eid = jnp.arange(N_EXP, dtype=jnp.int32)
        later = jnp.where(nonempty[None, :] & (eid[None, :] > eid[:, None]), eid[None, :], N_EXP)
        succ = jnp.min(later, axis=1)
        succ = jnp.where(succ < N_EXP, succ, -1).astype(jnp.int32)
        expert_slot = (order & 1).astype(jnp.int32)
        xb = _dispatch(h2, dest, (pad_start + counts).astype(jnp.int32),
                       (padded - counts).astype(jnp.int32), n_used, tm, nb)
        yb = _experts(xb, block_expert, expert_slot, succ, n_used, exp_w1[l],
                      exp_b1[l].reshape(N_EXP, 1, 2 * D_FF), exp_w2[l],
                      exp_b2[l].reshape(N_EXP, 1, D_MODEL), tm, nb)
        h = _combine(dest, h2, gates, _row(ln3_g[l]), _row(ln3_b[l]), yb)
    return h.reshape(batch, seq, D_MODEL)
```

```python
import functools
import math

import jax
import jax.numpy as jnp
from jax import lax
from jax.experimental import pallas as pl
from jax.experimental.pallas import tpu as pltpu

F32 = jnp.float32
BF16 = jnp.bfloat16

D_MODEL = 1024
HG_W = 512
HG_D = 128
HG_H = HG_W // HG_D
HG_CHUNK = 64
HG_SUB = 16
S5_W = 512
S5_N = 16
S5_G = S5_W // S5_N
S5_P = 64
S5_L = 64
S5_TB = 8
IN_COLS = 4 * HG_W + S5_W
N_MEM = 256
XA_H = 4
XA_D = D_MODEL // XA_H
XA_SCALE = XA_D ** -0.5
N_EXP = 32
TOP_K = 4
D_FF = D_MODEL
SWIGLU_LIMIT = 7.0
SWIGLU_ALPHA = 1.702
DEPTH = 1
ALPHA = (2 * DEPTH) ** 0.25
LN_EPS = 1e-5
RMS_EPS = 1e-6
LANES = 128

TM_IN = 512
TQ_MID = 512
TM_MOE = 256
TK_TOK = 256
COMBINE_GROUP = 128
VMEM_LIMIT = 56 * 1024 * 1024


def _layer_norm(x, g, b):
    mu = jnp.mean(x, axis=-1, keepdims=True)
    xc = x - mu
    var = jnp.mean(xc * xc, axis=-1, keepdims=True)
    return xc * lax.rsqrt(var + LN_EPS) * g + b


def _dot(a, b):
    return jnp.dot(a, b, preferred_element_type=F32)


def _dot_nt(a, b):
    return lax.dot_general(a, b, (((1,), (1,)), ((), ())), preferred_element_type=F32)


def _dot_tn(a, b):
    return lax.dot_general(a, b, (((0,), (0,)), ((), ())), preferred_element_type=F32)


def _split3(x):
    hi = x.astype(BF16)
    r1 = x - hi.astype(F32)
    mid = r1.astype(BF16)
    lo = (r1 - mid.astype(F32)).astype(BF16)
    return hi, mid, lo


def _ln_inproj_kernel(x_ref, g_ref, b_ref, w_ref, h_ref, qfig_ref, u_ref):
    h = _layer_norm(x_ref[...], g_ref[...], b_ref[...])
    h_ref[...] = h
    p = _dot(h.astype(BF16), w_ref[...])
    qfig_ref[...] = p[:, :4 * HG_W]
    u_ref[...] = p[:, 4 * HG_W:].astype(u_ref.dtype)


def _ln_inproj(x2, g, b, w_bf):
    t = x2.shape[0]
    tm = min(TM_IN, t)
    return pl.pallas_call(
        _ln_inproj_kernel,
        grid=(t // tm,),
        in_specs=[
            pl.BlockSpec((tm, D_MODEL), lambda i: (i, 0)),
            pl.BlockSpec((1, D_MODEL), lambda i: (0, 0)),
            pl.BlockSpec((1, D_MODEL), lambda i: (0, 0)),
            pl.BlockSpec((D_MODEL, IN_COLS), lambda i: (0, 0)),
        ],
        out_specs=[
            pl.BlockSpec((tm, D_MODEL), lambda i: (i, 0)),
            pl.BlockSpec((tm, 4 * HG_W), lambda i: (i, 0)),
            pl.BlockSpec((tm, S5_W), lambda i: (i, 0)),
        ],
        out_shape=[
            jax.ShapeDtypeStruct((t, D_MODEL), F32),
            jax.ShapeDtypeStruct((t, 4 * HG_W), F32),
            jax.ShapeDtypeStruct((t, S5_W), BF16),
        ],
        compiler_params=pltpu.CompilerParams(
            dimension_semantics=("arbitrary",), vmem_limit_bytes=VMEM_LIMIT),
        name="ln_inproj",
    )(x2, g, b, w_bf)


def _hgrn_kernel(q_ref, f_ref, i_ref, g_ref, lb_ref, ng_ref, o_ref, st_ref, k_s, cum_s):
    nseq = q_ref.shape[0]
    rows = nseq * HG_CHUNK

    @pl.when(pl.program_id(0) == 0)
    def _():
        st_ref[...] = jnp.zeros_like(st_ref)

    lb = lb_ref[...]
    f = lb + (1.0 - lb) * jax.nn.sigmoid(f_ref[...].reshape(rows, HG_W))
    k_s[...] = 1.0 - f
    r = lax.broadcasted_iota(jnp.int32, (rows, rows), 0)
    c = lax.broadcasted_iota(jnp.int32, (rows, rows), 1)
    tri = ((c <= r) & ((r // HG_CHUNK) == (c // HG_CHUNK))).astype(BF16)
    hi, mid, lo = _split3(jnp.log2(f))
    cum_s[...] = _dot(tri, hi) + _dot(tri, mid) + _dot(tri, lo)

    row_in_sub = lax.broadcasted_iota(jnp.int32, (HG_SUB, 1), 0)
    ng = ng_ref[...]
    for b in range(nseq):
        r0 = b * HG_CHUNK
        for h in range(HG_H):
            cols = slice(h * HG_D, (h + 1) * HG_D)
            qc = q_ref[b, :, cols]
            kc = k_s[r0:r0 + HG_CHUNK, cols]
            vc = i_ref[b, :, cols]
            cc = cum_s[r0:r0 + HG_CHUNK, cols]
            last = cum_s[r0 + HG_CHUNK - 1:r0 + HG_CHUNK, cols]
            st = st_ref[b, h]
            inter = _dot_nt((qc * jnp.exp2(cc)).astype(BF16), st.astype(BF16))
            vb = vc.astype(BF16)
            outs = []
            for blk in range(HG_CHUNK // HG_SUB):
                rs = slice(HG_SUB * blk, HG_SUB * (blk + 1))
                q_i = qc[rs]
                c_i = cc[rs]
                acc = inter[rs]
                if blk > 0:
                    n = HG_SUB * blk
                    c_ref = cum_s[r0 + n - 1:r0 + n, cols]
                    khat = (kc[:n] * jnp.exp2(c_ref - cc[:n])).astype(BF16)
                    qhat = (q_i * jnp.exp2(c_i - c_ref)).astype(BF16)
                    acc = acc + _dot(_dot_nt(qhat, khat).astype(BF16), vb[:n])
                for jj in range(HG_SUB):
                    srow = HG_SUB * blk + jj
                    c_s = cum_s[r0 + srow:r0 + srow + 1, cols]
                    k_row = k_s[r0 + srow:r0 + srow + 1, cols]
                    v_row = i_ref[b, srow:srow + 1, cols]
                    p = q_i * k_row * jnp.exp2(c_i - c_s)
                    w = jnp.sum(p, axis=-1, keepdims=True)
                    w = jnp.where(row_in_sub >= jj, w, 0.0)
                    acc = acc + w * v_row
                outs.append(acc)
            o = jnp.concatenate(outs, axis=0)
            kd = (kc * jnp.exp2(last - cc)).astype(BF16)
            st_ref[b, h] = st * jnp.exp2(last) + _dot_tn(vb, kd)
            o = o * lax.rsqrt(jnp.mean(o * o, axis=-1, keepdims=True) + RMS_EPS) * ng
            gg = g_ref[b, :, cols]
            o_ref[b, :, cols] = (o * (gg * jax.nn.sigmoid(gg))).astype(o_ref.dtype)


def _hgrn2(qfig, lb, ng, batch, seq):
    nj = seq // HG_CHUNK
    qfig3 = qfig.reshape(batch, seq, 4 * HG_W)

    def col_spec(cb):
        return pl.BlockSpec((batch, HG_CHUNK, HG_W), lambda j: (0, j, cb))

    out = pl.pallas_call(
        _hgrn_kernel,
        grid=(nj,),
        in_specs=[col_spec(0), col_spec(1), col_spec(2), col_spec(3),
                  pl.BlockSpec((1, HG_W), lambda j: (0, 0)),
                  pl.BlockSpec((1, HG_D), lambda j: (0, 0))],
        out_specs=pl.BlockSpec((batch, HG_CHUNK, HG_W), lambda j: (0, j, 0)),
        out_shape=jax.ShapeDtypeStruct((batch, seq, HG_W), BF16),
        scratch_shapes=[pltpu.VMEM((batch, HG_H, HG_D, HG_D), F32),
                        pltpu.VMEM((batch * HG_CHUNK, HG_W), F32),
                        pltpu.VMEM((batch * HG_CHUNK, HG_W), F32)],
        compiler_params=pltpu.CompilerParams(
            dimension_semantics=("arbitrary",), vmem_limit_bytes=VMEM_LIMIT),
        name="hgrn2",
    )(qfig3, qfig3, qfig3, qfig3, lb, ng)
    return out.reshape(batch * seq, HG_W)


def _s5_kernel(u_ref, k2_ref, hm_ref, gm_ref, ab_ref, d_ref, y_ref, *, nc):
    ub = u_ref[0]
    rows = ub.shape[0]
    nblk = S5_L // S5_TB
    blocks = []
    for dlt in reversed(range(nblk)):
        for s in range(S5_TB):
            off = S5_N * (S5_TB * dlt - s + S5_TB - 1)
            base = off // LANES * LANES
            win = k2_ref[0, :, base:base + 2 * LANES]
            if off != base:
                win = pltpu.roll(win, 2 * LANES - (off - base), axis=1)
            blocks.append(win[:, :LANES].astype(BF16))
    tt_rev = jnp.concatenate(blocks, axis=0)
    ys = []
    for bc in range(nblk):
        ys.append(_dot(ub[:, :(bc + 1) * LANES], tt_rev[(nblk - 1 - bc) * LANES:, :]))
    y = jnp.concatenate(ys, axis=1)
    x = _dot(ub, hm_ref[0])
    chunk_id = lax.broadcasted_iota(jnp.int32, (rows, 2 * S5_P), 0) % nc
    lev = 0
    while (1 << lev) < nc:
        sh = 1 << lev
        a = ab_ref[0, 2 * lev:2 * lev + 1, :]
        bv = ab_ref[0, 2 * lev + 1:2 * lev + 2, :]
        xs = jnp.where(chunk_id >= sh, pltpu.roll(x, sh, axis=0), 0.0)
        x = x + xs * a + pltpu.roll(xs, S5_P, axis=1) * bv
        lev += 1
    prev = jnp.where(chunk_id >= 1, pltpu.roll(x, 1, axis=0), 0.0)
    y = y + _dot(prev.astype(BF16), gm_ref[0])
    y = y + d_ref[0] * ub.astype(F32)
    y_ref[0] = jax.nn.gelu(y).astype(y_ref.dtype)


def _s5_tables(lam_re, lam_im, log_dt, b_re, b_im, c_re, c_im, d_skip, nc):
    hp = lax.Precision.HIGHEST
    lam_r, lam_i = lam_re.astype(F32), lam_im.astype(F32)
    dt = jnp.exp(log_dt.astype(F32))
    z_r, z_i = lam_r * dt, lam_i * dt

    def lam_pow(zr, zi, tau):
        mag = jnp.exp(zr * tau)
        return mag * jnp.cos(zi * tau), mag * jnp.sin(zi * tau)

    taus = jnp.arange(S5_L + 1, dtype=F32)[None, :, None]
    p_r, p_i = lam_pow(z_r[:, None, :], z_i[:, None, :], taus)
    x_r, x_i = p_r[:, 1] - 1.0, p_i[:, 1]
    den = lam_r * lam_r + lam_i * lam_i
    f_r = ((x_r * lam_r + x_i * lam_i) / den)[..., None]
    f_i = ((x_i * lam_r - x_r * lam_i) / den)[..., None]
    b_r, b_i = b_re.astype(F32), b_im.astype(F32)
    bb_r = f_r * b_r - f_i * b_i
    bb_i = f_r * b_i + f_i * b_r
    lane = jnp.arange(S5_L * S5_N)
    sel_t = (lane[None, :] // S5_N == jnp.arange(S5_L)[:, None]).astype(F32)
    sel_n = (lane[None, :] % S5_N == jnp.arange(S5_N)[:, None]).astype(F32)
    ce_r = jnp.einsum('gnp,nl->gpl', c_re.astype(F32), sel_n, precision=hp)
    ce_i = jnp.einsum('gnp,nl->gpl', c_im.astype(F32), sel_n, precision=hp)
    t_r = jnp.einsum('gtp,tl->gpl', p_r[:, :S5_L], sel_t, precision=hp)
    t_i = jnp.einsum('gtp,tl->gpl', p_i[:, :S5_L], sel_t, precision=hp)
    k2 = (jnp.einsum('gpm,gpl->gml', bb_r, ce_r * t_r - ce_i * t_i, precision=hp)
          - jnp.einsum('gpm,gpl->gml', bb_i, ce_r * t_i + ce_i * t_r, precision=hp))
    lead = (S5_TB - 1) * S5_N
    k2 = jnp.pad(k2, ((0, 0), (0, 0), (lead, LANES - lead)))
    q_r = p_r[:, S5_L - 1::-1][:, :, None, :]
    q_i = p_i[:, S5_L - 1::-1][:, :, None, :]
    bt_r = bb_r.transpose(0, 2, 1)[:, None]
    bt_i = bb_i.transpose(0, 2, 1)[:, None]
    hm = jnp.concatenate([q_r * bt_r - q_i * bt_i, q_r * bt_i + q_i * bt_r], axis=-1)
    hm = hm.reshape(S5_G, S5_L * S5_N, 2 * S5_P)
    e_r = jnp.einsum('gtp,tl->gpl', p_r[:, 1:], sel_t, precision=hp)
    e_i = jnp.einsum('gtp,tl->gpl', p_i[:, 1:], sel_t, precision=hp)
    gm =jnp.concatenate([ce_r * e_r - ce_i * e_i, -(ce_r * e_i + ce_i * e_r)], axis=1)
    rows = []
    lev = 0
    while (1 << lev) < nc:
        a, b = lam_pow(z_r, z_i, float(S5_L * (1 << lev)))
        rows.append(jnp.concatenate([a, a], axis=-1))
        rows.append(jnp.concatenate([-b, b], axis=-1))
        lev += 1
    if not rows:
        rows = [jnp.zeros((S5_G, 2 * S5_P), F32)] * 2
    ab = jnp.stack(rows, axis=1)
    dvec = jnp.einsum('gn,nl->gl', d_skip.astype(F32).reshape(S5_G, S5_N), sel_n, precision=hp)
    return k2, hm.astype(BF16), gm.astype(BF16), ab, dvec.reshape(S5_G, 1, S5_L * S5_N)


def _s5(u, tables, batch, seq):
    k2, hm, gm, ab, dvec = tables
    nc = seq // S5_L
    rows = batch * nc
    width = S5_L * S5_N
    ug = u.reshape(batch, nc, S5_L, S5_G, S5_N).transpose(3, 0, 1, 2, 4).reshape(S5_G, rows, width)
    nab = ab.shape[1]
    yg = pl.pallas_call(
        functools.partial(_s5_kernel, nc=nc),
        grid=(S5_G,),
        in_specs=[
            pl.BlockSpec((1, rows, width), lambda g: (g, 0, 0)),
            pl.BlockSpec((1, S5_N, k2.shape[2]), lambda g: (g, 0, 0)),
            pl.BlockSpec((1, width, 2 * S5_P), lambda g: (g, 0, 0)),
            pl.BlockSpec((1, 2 * S5_P, width), lambda g: (g, 0, 0)),
            pl.BlockSpec((1, nab, 2 * S5_P), lambda g: (g, 0, 0)),
            pl.BlockSpec((1, 1, width), lambda g: (g, 0, 0)),
        ],
        out_specs=pl.BlockSpec((1, rows, width), lambda g: (g, 0, 0)),
        out_shape=jax.ShapeDtypeStruct((S5_G, rows, width), BF16),
        compiler_params=pltpu.CompilerParams(
            dimension_semantics=("arbitrary",), vmem_limit_bytes=VMEM_LIMIT),
        name="s5",
    )(ug, k2, hm, gm, ab, dvec)
    return yg.reshape(S5_G, batch, nc, S5_L, S5_N).transpose(1, 2, 3, 0, 4).reshape(
        batch * seq, S5_W)


def _kv_kernel(m_ref, w_ref, k_ref, v_ref):
    kv = _dot(m_ref[...].astype(BF16), w_ref[...])
    k_ref[...] = kv[:, :D_MODEL].astype(BF16)
    v_ref[...] = kv[:, D_MODEL:].astype(BF16)


def _kv_proj(mem2, wkv_bf):
    rows = mem2.shape[0]
    tm = min(256, rows)
    return pl.pallas_call(
        _kv_kernel,
        grid=(rows // tm,),
        in_specs=[pl.BlockSpec((tm, D_MODEL), lambda i: (i, 0)),
                  pl.BlockSpec((D_MODEL, 2 * D_MODEL), lambda i: (0, 0))],
        out_specs=[pl.BlockSpec((tm, D_MODEL), lambda i: (i, 0)),
                   pl.BlockSpec((tm, D_MODEL), lambda i: (i, 0))],
        out_shape=[jax.ShapeDtypeStruct((rows, D_MODEL), BF16),
                   jax.ShapeDtypeStruct((rows, D_MODEL), BF16)],
        compiler_params=pltpu.CompilerParams(
            dimension_semantics=("arbitrary",), vmem_limit_bytes=VMEM_LIMIT),
        name="kv_proj",
    )(mem2, wkv_bf)


def _mid_kernel(oa_ref, y5_ref, h0_ref, k_ref, v_ref, wglu_ref, bglu_ref, wout_ref,
                g1_ref, b1_ref, wq_ref, wo_ref, g2_ref, b2_ref, wr_ref, br_ref,
                h2_ref, idx_ref, gate_ref, rank_ref, cnt_ref, run_ref):
    tq = oa_ref.shape[0]

    @pl.when((pl.program_id(0) == 0) & (pl.program_id(1) == 0))
    def _():
        run_ref[...] = jnp.zeros_like(run_ref)

    y5b = y5_ref[...]
    ob = y5b.astype(F32) * jax.nn.sigmoid(_dot(y5b, wglu_ref[...]) + bglu_ref[...])
    mix = _dot(oa_ref[...], wout_ref[:HG_W, :]) + _dot(ob.astype(BF16), wout_ref[HG_W:, :])
    h1 = _layer_norm(ALPHA * h0_ref[...] + mix, g1_ref[...], b1_ref[...])

    q = _dot(h1.astype(BF16), wq_ref[...])
    heads = []
    for hd in range(XA_H):
        cs = slice(hd * XA_D, (hd + 1) * XA_D)
        s = _dot_nt(q[:, cs].astype(BF16), k_ref[0, :, cs]) * XA_SCALE
        e = jnp.exp(s - jnp.max(s, axis=-1, keepdims=True))
        p = e / jnp.sum(e, axis=-1, keepdims=True)
        heads.append(_dot(p.astype(BF16), v_ref[0, :, cs]))
    xa = _dot(jnp.concatenate(heads, axis=1).astype(BF16), wo_ref[...])
    h2 = _layer_norm(ALPHA * h1 + xa, g2_ref[...], b2_ref[...])
    h2_ref[...] = h2

    h_hi, h_mid, _ = _split3(h2)
    w_hi, w_mid = wr_ref[0], wr_ref[1]
    logits = (_dot(h_hi, w_hi) + _dot(h_mid, w_hi) + _dot(h_hi, w_mid)) + br_ref[...]
    lane = lax.broadcasted_iota(jnp.int32, (tq, LANES), 1)
    lane_f = lane.astype(F32)
    work = jnp.where(lane < N_EXP, logits, -jnp.inf)
    vals, onehots = [], []
    for kk in range(TOP_K):
        mx = jnp.max(work, axis=-1, keepdims=True)
        sel = jnp.min(jnp.where(work == mx, lane_f, float(LANES)), axis=-1, keepdims=True)
        oh = lane_f == sel
        idx_ref[:, kk:kk + 1] = sel.astype(jnp.int32)
        vals.append(mx)
        onehots.append(oh)
        work = jnp.where(oh, -jnp.inf, work)
    es = [jnp.exp(v - vals[0]) for v in vals]
    den = es[0] + es[1] + es[2] + es[3]
    for kk in range(TOP_K):
        gate_ref[:, kk:kk + 1] = es[kk] / den

    member = jnp.zeros((tq, LANES), F32)
    for oh in onehots:
        member = member + oh.astype(F32)
    r = lax.broadcasted_iota(jnp.int32, (tq, tq), 0)
    c = lax.broadcasted_iota(jnp.int32, (tq, tq), 1)
    before = _dot((c < r).astype(BF16), member.astype(BF16)) + run_ref[...]
    for kk in range(TOP_K):
        rk = jnp.sum(jnp.where(onehots[kk], before, 0.0), axis=-1, keepdims=True)
        rank_ref[:, kk:kk + 1] = rk.astype(jnp.int32)
    run = run_ref[...] + jnp.sum(member, axis=0, keepdims=True)
    run_ref[...] = run
    cnt_ref[...] = run.astype(jnp.int32)


def _mid(oa, y5, h0, kmem, vmem, wglu, bglu, wout, g1, b1, wq, wo, g2, b2, wr3, br, batch, seq):
    t = oa.shape[0]
    tq = min(TQ_MID, seq)
    nj = seq // tq

    def row_spec(w):
        return pl.BlockSpec((tq, w), lambda b, j: (b * nj + j, 0))

    def full_spec(shape):
        nd = len(shape)
        return pl.BlockSpec(shape, lambda b, j: (0,) * nd)

    mem_spec = pl.BlockSpec((1, N_MEM, D_MODEL), lambda b, j: (b, 0, 0))
    return pl.pallas_call(
        _mid_kernel,
        grid=(batch, nj),
        in_specs=[row_spec(HG_W), row_spec(S5_W), row_spec(D_MODEL), mem_spec, mem_spec,
                  full_spec((S5_W, S5_W)), full_spec((1, S5_W)),
                  full_spec((D_MODEL, D_MODEL)), full_spec((1, D_MODEL)), full_spec((1, D_MODEL)),
                  full_spec((D_MODEL, D_MODEL)), full_spec((D_MODEL, D_MODEL)),
                  full_spec((1, D_MODEL)), full_spec((1, D_MODEL)),
                  full_spec((2, D_MODEL, LANES)), full_spec((1, LANES))],
        out_specs=[row_spec(D_MODEL), row_spec(TOP_K), row_spec(TOP_K), row_spec(TOP_K),
                   full_spec((1, LANES))],
        out_shape=[jax.ShapeDtypeStruct((t, D_MODEL), F32),
                   jax.ShapeDtypeStruct((t, TOP_K), jnp.int32),
                   jax.ShapeDtypeStruct((t, TOP_K), F32),
                   jax.ShapeDtypeStruct((t, TOP_K), jnp.int32),
                   jax.ShapeDtypeStruct((1, LANES), jnp.int32)],
        scratch_shapes=[pltpu.VMEM((1, LANES), F32)],
        compiler_params=pltpu.CompilerParams(
            dimension_semantics=("arbitrary", "arbitrary"), vmem_limit_bytes=VMEM_LIMIT),
        name="mid",
    )(oa, y5, h0, kmem, vmem, wglu, bglu, wout, g1, b1, wq, wo, g2, b2, wr3, br)


def _dest_kernel(ps_ref, idx_ref, rank_ref, o_ref):
    idx = idx_ref[...]
    acc = rank_ref[...]
    for e in range(N_EXP):
        acc = acc + jnp.where(idx == e, ps_ref[e], 0)
    o_ref[...] = acc


def _dest_rows(pad_start, top_idx, rank):
    n = top_idx.size
    shape = (n // LANES, LANES)
    grid_spec = pltpu.PrefetchScalarGridSpec(
        num_scalar_prefetch=1,
        grid=(1,),
        in_specs=[pl.BlockSpec(shape, lambda i, ps: (0, 0)),
                  pl.BlockSpec(shape, lambda i, ps: (0, 0))],
        out_specs=pl.BlockSpec(shape, lambda i, ps: (0, 0)),
    )
    dest = pl.pallas_call(
        _dest_kernel,
        grid_spec=grid_spec,
        out_shape=jax.ShapeDtypeStruct(shape, jnp.int32),
        name="dest_rows",
    )(pad_start, top_idx.reshape(shape), rank.reshape(shape))
    return dest.reshape(n)


def _row_copy(src, src_row, dst, dst_row, sem):
    return pltpu.make_async_copy(src.at[pl.ds(src_row, 1)], dst.at[pl.ds(dst_row, 1)], sem)


def _dispatch_kernel(pz_ref, pc_ref, nu_ref, dest_ref, h_ref, xb_ref, zbuf, sem, zsem, *, tm, nb):
    i = pl.program_id(0)
    tk = h_ref.shape[0]
    sub = 8
    nbits = (tm // sub).bit_length() - 1

    def zero_fill(start_not_wait):
        def act(cp):
            if start_not_wait:
                cp.start()
            else:
                cp.wait()

        def per_expert(e, carry):
            cnt = pc_ref[e]
            z0 = pz_ref[e]
            head = (-z0) & (sub - 1)
            for rr in range(sub - 1):
                @pl.when(rr < head)
                def _():
                    act(_row_copy(zbuf, 0, xb_ref, z0 + rr, zsem))
            za = z0 + head
            n_tiles = (cnt - head) >> 3
            for bit in range(nbits):
                size = sub << bit

                @pl.when(((n_tiles >> bit) & 1) == 1)
                def _():
                    off = pl.multiple_of(za + sub * (n_tiles & ((1 << bit) - 1)), sub)
                    act(pltpu.make_async_copy(zbuf.at[pl.ds(0, size)],
                                              xb_ref.at[pl.ds(off, size)], zsem))
            return carry

        lax.fori_loop(0, N_EXP, per_expert, 0)
        for jb in range(N_EXP):
            blk = nu_ref[0] + jb

            @pl.when(blk < nb)
            def _():
                off = pl.multiple_of(blk * tm, tm)
                act(pltpu.make_async_copy(zbuf, xb_ref.at[pl.ds(off, tm)], zsem))

    @pl.when(i == 0)
    def _():
        zbuf[...] = jnp.zeros_like(zbuf)
        zero_fill(True)

    def issue(tk_i, carry):
        for kk in range(TOP_K):
            _row_copy(h_ref, tk_i, xb_ref, dest_ref[tk_i * TOP_K + kk], sem).start(
                priority=kk % 2)
        return carry

    lax.fori_loop(0, tk, issue, 0, unroll=8)

    def drain(tk_i, carry):
        for kk in range(TOP_K):
            _row_copy(h_ref, 0, xb_ref, 0, sem).wait()
        return carry

    lax.fori_loop(0, tk, drain, 0, unroll=8)

    @pl.when(i == pl.num_programs(0) - 1)
    def _():
        zero_fill(False)


def _dispatch(h2, dest_flat, pad_zero_start, pad_cnt, n_used, tm, nb):
    t = h2.shape[0]
    tk = min(TK_TOK, t)
    grid_spec = pltpu.PrefetchScalarGridSpec(
        num_scalar_prefetch=3,
        grid=(t // tk,),
        in_specs=[pl.BlockSpec((tk * TOP_K,), lambda i, *_: (i,), memory_space=pltpu.SMEM),
                  pl.BlockSpec((tk, D_MODEL), lambda i, *_: (i, 0))],
        out_specs=pl.BlockSpec(memory_space=pl.ANY),
        scratch_shapes=[pltpu.VMEM((tm, D_MODEL), F32),
                        pltpu.SemaphoreType.DMA(()), pltpu.SemaphoreType.DMA(())],
    )
    return pl.pallas_call(
        functools.partial(_dispatch_kernel, tm=tm, nb=nb),
        grid_spec=grid_spec,
        out_shape=jax.ShapeDtypeStruct((nb * tm, D_MODEL), F32),
        compiler_params=pltpu.CompilerParams(
            dimension_semantics=("arbitrary",), vmem_limit_bytes=VMEM_LIMIT,
            has_side_effects=True),
        name="dispatch",
    )(pad_zero_start, pad_cnt, n_used, dest_flat, h2)


def _expert_kernel(be_ref, slot_ref, nxt_ref, nu_ref, xb_ref, w1_ref, b1_ref, w2_ref, b2_ref,
                   yb_ref, w1f, w2f, w1b, w2b, sem1, sem2):
    i = pl.program_id(0)
    used = i < nu_ref[0]
    e = be_ref[i]
    slot = slot_ref[e]
    new_expert = (i == 0) | (e != be_ref[jnp.maximum(i - 1, 0)])

    def fetch(expert, slot_):
        return (pltpu.make_async_copy(w1_ref.at[expert], w1f.at[slot_], sem1.at[slot_]),
                pltpu.make_async_copy(w2_ref.at[expert], w2f.at[slot_], sem2.at[slot_]))

    @pl.when(used & (i == 0))
    def _():
        for cp in fetch(e, slot):
            cp.start()

    @pl.when(used & new_expert)
    def _():
        for cp in fetch(e, slot):
            cp.wait()
        w1b[...] = w1f[slot].astype(BF16)
        w2b[...] = w2f[slot].astype(BF16)
        nxt = nxt_ref[e]

        @pl.when(nxt >= 0)
        def _():
            for cp in fetch(nxt, 1 - slot):
                cp.start()

    @pl.when(used)
    def _():
        x = xb_ref[...].astype(BF16)
        hh = _dot(x, w1b[...]) + b1_ref[0]
        x_glu = jnp.minimum(hh[:, :D_FF], SWIGLU_LIMIT)
        x_lin = jnp.clip(hh[:, D_FF:], -SWIGLU_LIMIT, SWIGLU_LIMIT)
        act = x_glu * jax.nn.sigmoid(SWIGLU_ALPHA * x_glu) * (x_lin + 1.0)
        yb_ref[...] = _dot(act.astype(BF16), w2b[...]) + b2_ref[0]

    @pl.when(i >= nu_ref[0])
    def _():
        yb_ref[...] = jnp.zeros_like(yb_ref)


def _experts(xb, block_expert, expert_slot, expert_next, n_used, w1, b1, w2, b2, tm, nb):
    def x_map(i, be, sl, nx, nu):
        return (jnp.minimum(i, nu[0] - 1), 0)

    def b_map(i, be, sl, nx, nu):
        return (be[i], 0, 0)

    grid_spec = pltpu.PrefetchScalarGridSpec(
        num_scalar_prefetch=4,
        grid=(nb,),
        in_specs=[pl.BlockSpec((tm, D_MODEL), x_map),
                  pl.BlockSpec(memory_space=pl.ANY),
                  pl.BlockSpec((1, 1, 2 * D_FF), b_map),
                  pl.BlockSpec(memory_space=pl.ANY),
                  pl.BlockSpec((1, 1, D_MODEL), b_map)],
        out_specs=pl.BlockSpec((tm, D_MODEL), lambda i, be, sl, nx, nu: (i, 0)),
        scratch_shapes=[pltpu.VMEM((2, D_MODEL, 2 * D_FF), F32),
                        pltpu.VMEM((2, D_FF, D_MODEL), F32),
                        pltpu.VMEM((D_MODEL, 2 * D_FF), BF16),
                        pltpu.VMEM((D_FF, D_MODEL), BF16),
                        pltpu.SemaphoreType.DMA((2,)), pltpu.SemaphoreType.DMA((2,))],
    )
    return pl.pallas_call(
        _expert_kernel,
        grid_spec=grid_spec,
        out_shape=jax.ShapeDtypeStruct((nb * tm, D_MODEL), F32),
        compiler_params=pltpu.CompilerParams(
            dimension_semantics=("arbitrary",), vmem_limit_bytes=VMEM_LIMIT),
        name="experts",
    )(block_expert, expert_slot, expert_next, n_used, xb, w1, b1, w2, b2)


def _combine_kernel(dest_ref, dnext_ref, h_ref, gate_ref, g_ref, b_ref, yb_ref, o_ref, ybuf, sem):
    i = pl.program_id(0)
    tk = h_ref.shape[0]
    slot = i % 2

    def gather(idx_ref, slot_):
        def issue(tk_i, carry):
            for kk in range(TOP_K):
                _row_copy(yb_ref, idx_ref[tk_i * TOP_K + kk], ybuf.at[slot_, kk], tk_i,
                          sem.at[slot_]).start(priority=kk % 2)
            return carry

        lax.fori_loop(0, tk, issue, 0, unroll=8)

    @pl.when(i == 0)
    def _():
        gather(dest_ref, 0)

    def drain(tk_i, carry):
        for kk in range(TOP_K):
            _row_copy(yb_ref, 0, ybuf.at[slot, kk], 0, sem.at[slot]).wait()
        return carry

    lax.fori_loop(0, tk, drain, 0, unroll=8)

    grp = COMBINE_GROUP

    def finish_rows(r0):
        rows = pl.ds(r0, grp)
        ff = jnp.zeros((grp, D_MODEL), F32)
        for kk in range(TOP_K):
            ff = ff + gate_ref[rows, kk:kk + 1] * ybuf[slot, kk, rows, :]
        o_ref[rows, :] = _layer_norm(ALPHA * h_ref[rows, :] + ff, g_ref[...], b_ref[...])

    more = i + 1 < pl.num_programs(0)

    @pl.when(more)
    def _():
        def body(j, carry):
            r0 = pl.multiple_of(j * grp, grp)
            finish_rows(r0)
            for u in range(grp):
                for kk in range(TOP_K):
                    _row_copy(yb_ref, dnext_ref[(r0 + u) * TOP_K + kk], ybuf.at[1 - slot, kk],
                              r0 + u, sem.at[1 - slot]).start(priority=kk % 2)
            return carry

        lax.fori_loop(0, tk // grp, body, 0)

    @pl.when(jnp.logical_not(more))
    def _():
        def body(j, carry):
            finish_rows(pl.multiple_of(j * grp, grp))
            return carry

        lax.fori_loop(0, tk // grp, body, 0)


def _combine(dest_flat, h2, gates, g3, b3, yb):
    t = h2.shape[0]
    tk = min(TK_TOK, t)
    last = t // tk - 1
    return pl.pallas_call(
        _combine_kernel,
        grid=(t // tk,),
        in_specs=[pl.BlockSpec((tk * TOP_K,), lambda i: (i,), memory_space=pltpu.SMEM),
                  pl.BlockSpec((tk * TOP_K,), lambda i: (jnp.minimum(i + 1, last),),
                               memory_space=pltpu.SMEM),
                  pl.BlockSpec((tk, D_MODEL), lambda i: (i, 0)),
                  pl.BlockSpec((tk, TOP_K), lambda i: (i, 0)),
                  pl.BlockSpec((1, D_MODEL), lambda i: (0, 0)),
                  pl.BlockSpec((1, D_MODEL), lambda i: (0, 0)),
                  pl.BlockSpec(memory_space=pl.ANY)],
        out_specs=pl.BlockSpec((tk, D_MODEL), lambda i: (i, 0)),
        out_shape=jax.ShapeDtypeStruct((t, D_MODEL), F32),
        scratch_shapes=[pltpu.VMEM((2, TOP_K, tk, D_MODEL), F32), pltpu.SemaphoreType.DMA((2,))],
        compiler_params=pltpu.CompilerParams(
            dimension_semantics=("arbitrary",), vmem_limit_bytes=VMEM_LIMIT),
        name="combine",
    )(dest_flat, dest_flat, h2, gates, g3, b3, yb)


def _row(v):
    return v.astype(F32).reshape(1, -1)


def kernel(x, mem, ln_in_g, ln_in_b, w_in, hgrn_lb_logits, hgrn_norm_g, s5_lam_re, s5_lam_im,
           s5_log_dt, s5_b_re, s5_b_im, s5_c_re, s5_c_im, s5_d, s5_w_glu, s5_b_glu, w_out,
           ln1_g, ln1_b, xa_w_q, xa_w_k, xa_w_v, xa_w_o, ln2_g, ln2_b, router_w, router_b,
           exp_w1, exp_b1, exp_w2, exp_b2, ln3_g, ln3_b):
    batch, seq, _ = x.shape
    t = batch * seq
    lb_all = jnp.cumsum(jax.nn.softmax(hgrn_lb_logits.astype(F32), axis=0), axis=0)
    h = x.reshape(t, D_MODEL)
    assert DEPTH == 1 and w_in.shape[0] == 1
    for l in range(DEPTH):
        h0, qfig, u = _ln_inproj(h, _row(ln_in_g), _row(ln_in_b), w_in[l].astype(BF16))
        o_a = _hgrn2(qfig, lb_all[l].reshape(1, HG_W), _row(hgrn_norm_g[l]), batch, seq)
        s5t = _s5_tables(s5_lam_re[l], s5_lam_im[l], s5_log_dt[l], s5_b_re[l], s5_b_im[l],
                         s5_c_re[l], s5_c_im[l], s5_d[l], seq // S5_L)
        y5 = _s5(u, s5t, batch, seq)
        wkv = jnp.concatenate([xa_w_k[l], xa_w_v[l]], axis=1).astype(BF16)
        kmem, vmem = _kv_proj(mem.reshape(batch * N_MEM, D_MODEL), wkv)
        kmem = kmem.reshape(batch, N_MEM, D_MODEL)
        vmem = vmem.reshape(batch, N_MEM, D_MODEL)
        wr = jnp.pad(router_w[l].astype(F32), ((0, 0), (0, LANES - N_EXP)))
        wr3 = jnp.stack(_split3(wr)[:2], axis=0)
        br = jnp.pad(router_b[l].astype(F32), (0, LANES - N_EXP)).reshape(1, LANES)
        h2, top_idx, gates, rank, cnt = _mid(
            o_a, y5, h0, kmem, vmem, s5_w_glu[l].astype(BF16), _row(s5_b_glu[l]),
            w_out[l].astype(BF16), _row(ln1_g[l]), _row(ln1_b[l]), xa_w_q[l].astype(BF16),
            xa_w_o[l].astype(BF16), _row(ln2_g[l]), _row(ln2_b[l]), wr3, br, batch, seq)

        tm = min(TM_MOE, t)
        nb = (t * TOP_K) // tm + N_EXP
        counts = cnt[0, :N_EXP]
        padded = (counts + tm - 1) // tm * tm
        pad_end = jnp.cumsum(padded)
        pad_start = pad_end - padded
        dest = _dest_rows(pad_start.astype(jnp.int32), top_idx, rank)
        n_used = (pad_end[-1:] // tm).astype(jnp.int32)
        block_start = jnp.arange(nb, dtype=jnp.int32) * tm
        block_expert = jnp.sum(
            pad_end[None, :] <= jnp.minimum(block_start, pad_end[-1] - tm)[:, None],
            axis=1).astype(jnp.int32)
        nonempty = counts > 0
        order = jnp.cumsum(nonempty.astype(jnp.int32)) - 1
        eid = jnp.arange(N_EXP, dtype=jnp.int32)
        later = jnp.where(nonempty[None, :] & (eid[None, :] > eid[:, None]), eid[None, :], N_EXP)
        succ = jnp.min(later, axis=1)
        succ = jnp.where(succ < N_EXP, succ, -1).astype(jnp.int32)
        expert_slot = (order & 1).astype(jnp.int32)
        xb = _dispatch(h2, dest, (pad_start + counts).astype(jnp.int32),
                       (padded - counts).astype(jnp.int32), n_used, tm, nb)
        yb = _experts(xb, block_expert, expert_slot, succ, n_used, exp_w1[l],
                      exp_b1[l].reshape(N_EXP, 1, 2 * D_FF), exp_w2[l],
                      exp_b2[l].reshape(N_EXP, 1, D_MODEL), tm, nb)
        h = _combine(dest, h2, gates, _row(ln3_g[l]), _row(ln3_b[l]), yb)
    return h.reshape(batch, seq, D_MODEL)
```
